```python
import math
import jax
import jax.numpy as jnp
from jax import lax
import numpy as np

D_MODEL = 1024
BATCH = 2
SEQ = 16384
DEPTH = 4
DEC_BATCH = 8
DEC_SEQ = 8192
PAST_LEN = 128

N_MOD = 6
EPS = 1e-6
POOL_WINDOWS = (2, 4, 8, 16)
POOL_GROUP = 3 * D_MODEL // 32
POOL_WIDTH = 4 * POOL_GROUP
HYENA_WIDTH = 3 * D_MODEL // 8
HYENA_SHORT = 3
FILTER_EMB = 33
FILTER_ORDER = 64
HYENA_FAST_DECAY = 0.3
HYENA_SLOW_DECAY = 1.5
HYENA_TARGET = 1e-2
HEAD_DIM = 64
ATTN_GROUPS = ((128, 1), (512, 4), (2048, 16))
HEADS_PER_GROUP = 4
ATTN_HEADS = HEADS_PER_GROUP * len(ATTN_GROUPS)
ATTN_WIDTH = ATTN_HEADS * HEAD_DIM
ATTN_OUT = HEADS_PER_GROUP * HEAD_DIM
ROPE_THETA = 10000.0
NEG_BIG = -1e30
N_BRANCH = 3
OFF_HYENA = POOL_WIDTH
OFF_ATTN = OFF_HYENA + 3 * HYENA_WIDTH
OFF_GATE = OFF_ATTN + 3 * ATTN_WIDTH
IN_COLS = OFF_GATE + N_BRANCH * D_MODEL
N_EXPERTS = 32
TOP_K = 4
D_EXPERT = D_MODEL
SWIGLU_LIMIT = 7.0
SWIGLU_ALPHA = 1.702
MOE_BLOCK = 128

kernel_name = 'hybrid_pool_hyena_dilated_moe_encoder'


def rms_norm(x, g):
    x32 = x.astype(jnp.float32)
    y = x32 * lax.rsqrt(jnp.mean(x32 * x32, axis=-1, keepdims=True) + EPS)
    return (y * g.astype(jnp.float32)).astype(x.dtype)


def multiscale_pool(a, pool_w, pool_scale):
    B, L, _ = a.shape
    a32 = a.astype(jnp.float32)
    cs = jnp.concatenate([jnp.zeros((B, 1, POOL_WIDTH), jnp.float32), lax.cumsum(a32, axis=1)], axis=1)
    pos = jnp.arange(L)
    outs = []
    for gi, w in enumerate(POOL_WINDOWS):
        left = w // 2
        right = w - 1 - left
        lo, hi = gi * POOL_GROUP, (gi + 1) * POOL_GROUP
        csg = jnp.pad(cs[:, :, lo:hi], ((0, 0), (left, right), (0, 0)), mode='edge')
        win_sum = csg[:, w:w + L] - csg[:, :L]
        count = (jnp.minimum(pos + right, L - 1) - jnp.maximum(pos - left, 0) + 1).astype(jnp.float32)
        pooled = win_sum / count[None, :, None] - a32[:, :, lo:hi]
        outs.append(jnp.einsum('blg,gh->blh', pooled.astype(a.dtype), pool_w[gi]))
    return jnp.concatenate(outs, axis=-1) * pool_scale


def short_conv(u, w, b):
    C = u.shape[-1]
    y = lax.conv_general_dilated(
        u, w.reshape(HYENA_SHORT, 1, C).astype(u.dtype), window_strides=(1,),
        padding=((HYENA_SHORT // 2, HYENA_SHORT // 2),),
        dimension_numbers=('NWC', 'WIO', 'NWC'), feature_group_count=C)
    return y + b


def hyena_filter(L, w1, b1, w2, b2, w3, b3, w4, freq):
    f32 = jnp.float32
    t = jnp.linspace(0.0, 1.0, L, dtype=f32)[:, None]
    bands = (FILTER_EMB - 1) // 2
    omega = (2.0 * math.pi / L) * jnp.arange(L, dtype=f32)[:, None]
    fr = jnp.linspace(1e-4, bands - 1, bands, dtype=f32)[None, :]
    z = jnp.concatenate([t, jnp.cos(fr * omega), -jnp.sin(fr * omega)], axis=-1)
    fq = freq.astype(f32)
    h = jnp.sin(fq * (z @ w1.astype(f32) + b1.astype(f32)))
    h = jnp.sin(fq * (h @ w2.astype(f32) + b2.astype(f32)))
    h = jnp.sin(fq * (h @ w3.astype(f32) + b3.astype(f32)))
    h = h @ w4.astype(f32)
    max_decay = math.log(HYENA_TARGET) / HYENA_FAST_DECAY
    min_decay = math.log(HYENA_TARGET) / HYENA_SLOW_DECAY
    deltas = jnp.linspace(min_decay, max_decay, HYENA_WIDTH, dtype=f32)
    decay = jnp.exp(-t * jnp.abs(deltas)[None, :])
    h_fwd = h[:, :HYENA_WIDTH] * decay
    h_bwd = h[:, HYENA_WIDTH:] * decay
    k2 = jnp.concatenate([h_fwd, jnp.zeros((1, HYENA_WIDTH), f32), h_bwd[:0:-1]], axis=0)
    return k2 / jnp.sum(jnp.abs(k2), axis=0, keepdims=True)


def hyena_mixer(u, conv_w, conv_b, w1, b1, w2, b2, w3, b3, w4, freq, hy_d):
    B, L, _ = u.shape
    u = short_conv(u, conv_w, conv_b)
    x0, x1, v = jnp.split(u, 3, axis=-1)
    k2 = hyena_filter(L, w1, b1, w2, b2, w3, b3, w4, freq)
    z = (x1 * v).astype(jnp.float32)
    spec = jnp.fft.rfft(z, n=2 * L, axis=1) * jnp.fft.rfft(k2, n=2 * L, axis=0)[None]
    y = jnp.fft.irfft(spec, n=2 * L, axis=1)[:, :L] + z * hy_d.astype(jnp.float32)
    return (x0.astype(jnp.float32) * y).astype(u.dtype)


def rope_tables(L):
    inv = ROPE_THETA ** (-jnp.arange(0, HEAD_DIM, 2, dtype=jnp.float32) / HEAD_DIM)
    ang = jnp.arange(L, dtype=jnp.float32)[:, None] * inv[None, :]
    ang = jnp.concatenate([ang, ang], axis=-1)
    return jnp.cos(ang)[None, :, None, :], jnp.sin(ang)[None, :, None, :]


def norm_rope(t, g, cos, sin):
    t32 = t.astype(jnp.float32)
    t32 = t32 * lax.rsqrt(jnp.mean(t32 * t32, axis=-1, keepdims=True) + EPS) * g.astype(jnp.float32)
    t1, t2 = jnp.split(t32, 2, axis=-1)
    return (t32 * cos + jnp.concatenate([-t2, t1], axis=-1) * sin).astype(t.dtype)


def dilated_band_attention(q, k, v, dil, half_steps):
    B, L, H, E = q.shape
    M = L // dil
    QB = half_steps
    nb = -(-M // QB)
    Mp = nb * QB

    def to_res(t):
        t = jnp.moveaxis(t.reshape(B, M, dil, H, E), 2, 1)
        return jnp.pad(t, ((0, 0), (0, 0), (0, Mp - M), (0, 0), (0, 0)))

    def windows(t):
        tp = jnp.pad(to_res(t), ((0, 0), (0, 0), (QB, QB), (0, 0), (0, 0))).reshape(B, dil, nb + 2, QB, H, E)
        return jnp.concatenate([tp[:, :, :-2], tp[:, :, 1:-1], tp[:, :, 2:]], axis=3)

    qr = to_res(q).reshape(B, dil, nb, QB, H, E)
    kw = windows(k)
    vw = windows(v)
    s = jnp.einsum('brnqhe,brnkhe->brnhqk', qr, kw, preferred_element_type=jnp.float32) * (E ** -0.5)
    rel = jnp.arange(3 * QB)[None, :] - QB - jnp.arange(QB)[:, None]
    key_pos = jnp.arange(nb)[:, None] * QB + jnp.arange(3 * QB)[None, :] - QB
    mask = (jnp.abs(rel) <= half_steps)[None] & ((key_pos >= 0) & (key_pos < M))[:, None, :]
    s = jnp.where(mask[None, None, :, None], s, NEG_BIG)
    mx = jnp.max(s, axis=-1, keepdims=True)
    p = jnp.exp(s - mx)
    den = jnp.sum(p, axis=-1)
    o = jnp.einsum('brnhqk,brnkhe->brnqhe', p.astype(v.dtype), vw, preferred_element_type=jnp.float32)
    den_q = jnp.swapaxes(den, 3, 4)
    o = o / den_q[..., None]
    lse = jnp.swapaxes(mx[..., 0], 3, 4) + jnp.log(den_q)

    def from_res(t):
        t = t.reshape((B, dil, Mp) + t.shape[4:])[:, :, :M]
        t = jnp.moveaxis(t, 1, 2)
        return t.reshape((B, L) + t.shape[3:])

    return from_res(o), from_res(lse)


def dilated_attention(u, q_norm_g, k_norm_g, cos, sin):
    B, L, _ = u.shape
    q, k, v = [t.reshape(B, L, ATTN_HEADS, HEAD_DIM) for t in jnp.split(u, 3, axis=-1)]
    q = norm_rope(q, q_norm_g, cos, sin)
    k = norm_rope(k, k_norm_g, cos, sin)
    outs, lses = [], []
    for gi, (window, dil) in enumerate(ATTN_GROUPS):
        hs = slice(gi * HEADS_PER_GROUP, (gi + 1) * HEADS_PER_GROUP)
        o, lse = dilated_band_attention(q[:, :, hs], k[:, :, hs], v[:, :, hs], dil, window // 2 // dil)
        outs.append(o)
        lses.append(lse)
    wts = jax.nn.softmax(jnp.stack(lses, axis=0), axis=0)
    o = jnp.sum(wts[..., None] * jnp.stack(outs, axis=0), axis=0)
    return o.reshape(B, L, ATTN_OUT).astype(u.dtype)


def moe(h, w_router, b_router, w_mlp1, b_mlp1, w_mlp2, b_mlp2):
    B, L, D = h.shape
    T = B * L
    hf = h.reshape(T, D)
    logits = (hf @ w_router).astype(jnp.float32) + b_router.astype(jnp.float32)
    top_v, top_i = lax.top_k(logits, TOP_K)
    gate_w = jax.nn.softmax(top_v, axis=-1)
    n_assign = T * TOP_K
    flat_e = top_i.reshape(-1)
    flat_tok = jnp.arange(n_assign, dtype=jnp.int32) // TOP_K
    flat_w = gate_w.reshape(-1)
    order = jnp.argsort(flat_e)
    se, stok, sw = flat_e[order], flat_tok[order], flat_w[order]
    counts = jnp.bincount(flat_e, length=N_EXPERTS)
    starts = jnp.cumsum(counts) - counts
    pcounts = (counts + MOE_BLOCK - 1) // MOE_BLOCK * MOE_BLOCK
    pends = jnp.cumsum(pcounts)
    pstarts = pends - pcounts
    dest = pstarts[se] + (jnp.arange(n_assign) - starts[se])
    R = -(-n_assign // MOE_BLOCK) * MOE_BLOCK + N_EXPERTS * MOE_BLOCK
    nblk = R // MOE_BLOCK
    row_tok = jnp.full((R,), T, jnp.int32).at[dest].set(stok)
    row_w = jnp.zeros((R,), h.dtype).at[dest].set(sw.astype(h.dtype))
    blk_e = jnp.minimum(jnp.searchsorted(pends, jnp.arange(nblk) * MOE_BLOCK, side='right'), N_EXPERTS - 1)
    hpad = jnp.concatenate([hf, jnp.zeros((1, D), hf.dtype)], axis=0)

    def run_block(args):
        tok, e = args
        xb = hpad[tok]
        u = xb @ w_mlp1[e] + b_mlp1[e]
        x_glu = jnp.minimum(u[:, :D_EXPERT], SWIGLU_LIMIT)
        x_lin = jnp.clip(u[:, D_EXPERT:], -SWIGLU_LIMIT, SWIGLU_LIMIT)
        act = x_glu * jax.nn.sigmoid(SWIGLU_ALPHA * x_glu) * (x_lin + 1.0)
        return act @ w_mlp2[e] + b_mlp2[e]

    out = lax.map(run_block, (row_tok.reshape(nblk, MOE_BLOCK), blk_e)).reshape(R, D)
    y = jnp.zeros((T + 1, D), h.dtype).at[row_tok].add(out * row_w[:, None])[:T]
    return y.reshape(B, L, D)


def token_mixer(h, p, cos, sin):
    B, L, D = h.shape
    proj = h @ p['w_in'] + p['b_in']
    ya = multiscale_pool(proj[..., :OFF_HYENA], p['pool_w'], p['pool_scale'])
    yb = hyena_mixer(proj[..., OFF_HYENA:OFF_ATTN], p['hy_conv_w'], p['hy_conv_b'],
                     p['filt_w1'], p['filt_b1'], p['filt_w2'], p['filt_b2'], p['filt_w3'], p['filt_b3'],
                     p['filt_w4'], p['filt_freq'], p['hy_d'])
    yc = dilated_attention(proj[..., OFF_ATTN:OFF_GATE], p['q_norm_g'], p['k_norm_g'], cos, sin)
    gates = jax.nn.sigmoid(proj[..., OFF_GATE:]).reshape(B, L, N_BRANCH, D)
    merged = (gates[:, :, 0] * (ya @ p['w_branch_a'])
              + gates[:, :, 1] * (yb @ p['w_branch_b'])
              + gates[:, :, 2] * (yc @ p['w_branch_c']))
    return merged @ p['w_o']


def encoder_layer(x, c, p, cos, sin):
    B = x.shape[0]
    mod = (jax.nn.silu(c) @ p['ada_w'] + p['ada_b']).reshape(B, N_MOD, 1, D_MODEL)
    sh1, sc1, gt1, sh2, sc2, gt2 = [mod[:, i] for i in range(N_MOD)]
    h = rms_norm(x, p['norm1_g']) * (1.0 + sc1) + sh1
    x = x + gt1 * token_mixer(h, p, cos, sin)
    h = rms_norm(x, p['norm2_g']) * (1.0 + sc2) + sh2
    x = x + gt2 * moe(h, p['w_router'], p['b_router'], p['w_mlp1'], p['b_mlp1'], p['w_mlp2'], p['b_mlp2'])
    return x


def setup_inputs(seed: int = 0) -> dict:
    key = jax.random.key(seed)
    ks = iter(jax.random.split(key, 48))

    def nrm(shape, scale):
        return jax.random.normal(next(ks), shape, jnp.float32) * scale

    D, HW, F, E, NL = D_MODEL, HYENA_WIDTH, D_EXPERT, N_EXPERTS, DEPTH
    return {
        'x_prompt': nrm((BATCH, SEQ, D), 1.0),
        'x_sample': nrm((DEC_BATCH, DEC_SEQ, D), 1.0),
        'c_prompt': nrm((BATCH, D), 1.0),
        'c_sample': nrm((DEC_BATCH, D), 1.0),
        'ada_w': nrm((NL, D, N_MOD * D), 0.5 * D ** -0.5),
        'ada_b': nrm((NL, N_MOD * D), 0.01),
        'norm1_g': 1.0 + nrm((NL, D), 0.02),
        'w_in': nrm((NL, D, IN_COLS), D ** -0.5),
        'b_in': nrm((NL, IN_COLS), 0.01),
        'pool_w': nrm((NL, len(POOL_WINDOWS), POOL_GROUP, POOL_GROUP), POOL_GROUP ** -0.5),
        'pool_scale': 1.0 + nrm((NL, POOL_WIDTH), 0.02),
        'hy_conv_w': nrm((NL, HYENA_SHORT, 3 * HW), 0.5),
        'hy_conv_b': nrm((NL, 3 * HW), 0.01),
        'filt_w1': nrm((NL, FILTER_EMB, FILTER_ORDER), FILTER_EMB ** -0.5),
        'filt_b1': nrm((NL, FILTER_ORDER), 0.1),
        'filt_w2': nrm((NL, FILTER_ORDER, FILTER_ORDER), FILTER_ORDER ** -0.5),
        'filt_b2': nrm((NL, FILTER_ORDER), 0.1),
        'filt_w3': nrm((NL, FILTER_ORDER, FILTER_ORDER), FILTER_ORDER ** -0.5),
        'filt_b3': nrm((NL, FILTER_ORDER), 0.1),
        'filt_w4': nrm((NL, FILTER_ORDER, 2 * HW), FILTER_ORDER ** -0.5),
        'filt_freq': 1.0 + nrm((NL, FILTER_ORDER), 0.1),
        'hy_d': nrm((NL, HW), 0.5),
        'q_norm_g': 1.0 + nrm((NL, HEAD_DIM), 0.02),
        'k_norm_g': 1.0 + nrm((NL, HEAD_DIM), 0.02),
        'w_branch_a': nrm((NL, POOL_WIDTH, D), POOL_WIDTH ** -0.5),
        'w_branch_b': nrm((NL, HW, D), HW ** -0.5),
        'w_branch_c': nrm((NL, ATTN_OUT, D), ATTN_OUT ** -0.5),
        'w_o': nrm((NL, D, D), D ** -0.5),
        'norm2_g': 1.0 + nrm((NL, D), 0.02),
        'w_router': nrm((NL, D, E), D ** -0.5),
        'b_router': nrm((NL, E), 0.01),
        'w_mlp1': nrm((NL, E, D, 2 * F), D ** -0.5),
        'b_mlp1': nrm((NL, E, 2 * F), 0.01),
        'w_mlp2': nrm((NL, E, F, D), F ** -0.5),
        'b_mlp2': nrm((NL, E, D), 0.01),
    }


def reference(x_prompt, x_sample, c_prompt, c_sample, ada_w, ada_b, norm1_g, w_in, b_in,
              pool_w, pool_scale, hy_conv_w, hy_conv_b, filt_w1, filt_b1, filt_w2, filt_b2,
              filt_w3, filt_b3, filt_w4, filt_freq, hy_d, q_norm_g, k_norm_g,
              w_branch_a, w_branch_b, w_branch_c, w_o, norm2_g, w_router, b_router,
              w_mlp1, b_mlp1, w_mlp2, b_mlp2):
    cos_p, sin_p = rope_tables(x_prompt.shape[1])
    cos_s, sin_s = rope_tables(x_sample.shape[1])
    y_prompt = x_prompt
    y_sample = x_sample
    for l in range(DEPTH):
        p = {
            'ada_w': ada_w[l], 'ada_b': ada_b[l], 'norm1_g': norm1_g[l],
            'w_in': w_in[l], 'b_in': b_in[l],
            'pool_w': pool_w[l], 'pool_scale': pool_scale[l],
            'hy_conv_w': hy_conv_w[l], 'hy_conv_b': hy_conv_b[l],
            'filt_w1': filt_w1[l], 'filt_b1': filt_b1[l], 'filt_w2': filt_w2[l], 'filt_b2': filt_b2[l],
            'filt_w3': filt_w3[l], 'filt_b3': filt_b3[l], 'filt_w4': filt_w4[l], 'filt_freq': filt_freq[l],
            'hy_d': hy_d[l], 'q_norm_g': q_norm_g[l], 'k_norm_g': k_norm_g[l],
            'w_branch_a': w_branch_a[l], 'w_branch_b': w_branch_b[l], 'w_branch_c': w_branch_c[l],
            'w_o': w_o[l], 'norm2_g': norm2_g[l], 'w_router': w_router[l], 'b_router': b_router[l],
            'w_mlp1': w_mlp1[l], 'b_mlp1': b_mlp1[l], 'w_mlp2': w_mlp2[l], 'b_mlp2': b_mlp2[l],
        }
        y_prompt = encoder_layer(y_prompt, c_prompt, p, cos_p, sin_p)
        y_sample = encoder_layer(y_sample, c_sample, p, cos_s, sin_s)
    return (y_prompt, y_sample)
```

```python
import functools
import math

import numpy as np
import jax
import jax.numpy as jnp
from jax import lax
from jax.experimental import pallas as pl
from jax.experimental.pallas import tpu as pltpu

F32 = jnp.float32
BF16 = jnp.bfloat16
I32 = jnp.int32

D_MODEL = 1024
N_MOD = 6
EPS = 1e-6
POOL_WINDOWS = (2, 4, 8, 16)
POOL_GROUP = 96
POOL_WIDTH = 384
POOL_HALO = 16
HYENA_WIDTH = 384
FILTER_ORDER = 64
FILTER_BANDS = 16
HYENA_FAST_DECAY = 0.3
HYENA_SLOW_DECAY = 1.5
HYENA_TARGET = 1e-2
HEAD_DIM = 64
ATTN_GROUPS = ((128, 1), (512, 4), (2048, 16))
HEADS_PER_GROUP = 4
ATTN_WIDTH = 768
ATTN_OUT = 256
ATTN_HALF = 64
ROPE_THETA = 10000.0
NEG_BIG = -1e30
OFF_HYENA = POOL_WIDTH
OFF_ATTN = OFF_HYENA + 3 * HYENA_WIDTH
OFF_GATE = OFF_ATTN + 3 * ATTN_WIDTH
IN_COLS = OFF_GATE + 3 * D_MODEL
N_EXPERTS = 32
TOP_K = 4
SWIGLU_LIMIT = 7.0
SWIGLU_ALPHA = 1.702
LANES = 128
MOE_ROWS = 256
VMEM_LIMIT = 56 * 1024 * 1024


def _cp(sem, vmem=None):
    return pltpu.CompilerParams(dimension_semantics=sem, vmem_limit_bytes=vmem)


def _split(x):
    hi = x.astype(BF16)
    lo = (x - hi.astype(F32)).astype(BF16)
    return hi, lo


def _dot(a, b):
    return jnp.dot(a, b, preferred_element_type=F32)


def _dot3(a_hi, a_lo, b):
    b_hi, b_lo = _split(b)
    return _dot(a_hi, b_hi) + _dot(a_hi, b_lo) + _dot(a_lo, b_hi)


def _dot3r(a, b_hi, b_lo):
    a_hi, a_lo = _split(a)
    return _dot(a_hi, b_hi) + _dot(a_lo, b_hi) + _dot(a_hi, b_lo)


def _fft_factors(n):
    n2 = 128 if n >= 16384 else 64
    return n // n2, n2


def _ada_kernel(c_ref, w_ref, b_ref, o_ref):
    c = c_ref[...]
    s = c * jax.nn.sigmoid(c)
    w_hi, w_lo = _split(w_ref[0])
    o_ref[0] = _dot3r(s, w_hi, w_lo) + b_ref[0]


def ada_modulation(c_all, ada_w, ada_b):
    nl, d, n = ada_w.shape
    bp = c_all.shape[0]
    tn = 1536
    return pl.pallas_call(
        _ada_kernel,
        out_shape=jax.ShapeDtypeStruct((nl, bp, n), F32),
        grid=(nl, n // tn),
        in_specs=[pl.BlockSpec((bp, d), lambda l, j: (0, 0)),
                  pl.BlockSpec((1, d, tn), lambda l, j: (l, 0, j)),
                  pl.BlockSpec((1, 1, tn), lambda l, j: (l, 0, j))],
        out_specs=pl.BlockSpec((1, bp, tn), lambda l, j: (l, 0, j)),
        compiler_params=_cp(("parallel", "parallel"), VMEM_LIMIT),
        name="ada_modulation",
    )(c_all, ada_w, ada_b.reshape(nl, 1, n))


IN_SPLITS = ((0, POOL_WIDTH), (OFF_HYENA, 3 * HYENA_WIDTH), (OFF_ATTN, 3 * ATTN_WIDTH),
             (OFF_GATE, 3 * D_MODEL))
IN_CHUNK = 768


def _modulated_norm(x, g, scale, shift):
    ms = jnp.mean(x * x, axis=-1, keepdims=True)
    return (x * lax.rsqrt(ms + EPS) * g) * (1.0 + scale) + shift


def _inproj_kernel(x_ref, mod_ref, g_ref, w_ref, b_ref, a_ref, hy_ref, qkv_ref, gt_ref):
    h = _modulated_norm(x_ref[0], g_ref[...], mod_ref[0, 1:2, :], mod_ref[0, 0:1, :])
    hb = h.astype(BF16)
    for (off, width), o_ref in zip(IN_SPLITS, (a_ref, hy_ref, qkv_ref, gt_ref)):
        for c0 in range(0, width, IN_CHUNK):
            c1 = min(c0 + IN_CHUNK, width)
            o_ref[0, :, c0:c1] = (_dot(hb, w_ref[:, off + c0:off + c1])
                                  + b_ref[:, off + c0:off + c1])


def in_projection(x, mod, g1, w_in_bf, b_in):
    b, l, d = x.shape
    tm = 256
    outs = tuple(jax.ShapeDtypeStruct((b, l, w), F32) for _, w in IN_SPLITS)
    return pl.pallas_call(
        _inproj_kernel,
        out_shape=outs,
        grid=(b, l // tm),
        in_specs=[pl.BlockSpec((1, tm, d), lambda bi, i: (bi, i, 0)),
                  pl.BlockSpec((1, N_MOD, d), lambda bi, i: (bi, 0, 0)),
                  pl.BlockSpec((1, d), lambda bi, i: (0, 0)),
                  pl.BlockSpec((d, IN_COLS), lambda bi, i: (0, 0), pipeline_mode=pl.Buffered(1)),
                  pl.BlockSpec((1, IN_COLS), lambda bi, i: (0, 0))],
        out_specs=tuple(pl.BlockSpec((1, tm, w), lambda bi, i: (bi, i, 0)) for _, w in IN_SPLITS),
        compiler_params=_cp(("parallel", "parallel"), VMEM_LIMIT),
        name="in_projection",
    )(x, mod, g1.reshape(1, d), w_in_bf, b_in.reshape(1, IN_COLS))


def _hyena_pre_kernel(cur_ref, prev_ref, next_ref, w_ref, b_ref, x0_ref, z_ref):
    i = pl.program_id(1)
    last = pl.num_programs(1) - 1
    cur = cur_ref[0]
    tm = cur.shape[0]
    prev_row = jnp.where(i > 0, prev_ref[0, 7:8, :], 0.0)
    next_row = jnp.where(i < last, next_ref[0, 0:1, :], 0.0)
    row = lax.broadcasted_iota(I32, (tm, 1), 0)
    up = jnp.where(row == 0, prev_row, pltpu.roll(cur, 1, 0))
    dn = jnp.where(row == tm - 1, next_row, pltpu.roll(cur, tm - 1, 0))
    y = w_ref[0:1, :] * up + w_ref[1:2, :] * cur + w_ref[2:3, :] * dn + b_ref[...]
    c = HYENA_WIDTH
    x0_ref[0] = y[:, :c]
    z_ref[0] = y[:, c:2 * c] * y[:, 2 * c:]


def hyena_pre(hy, conv_w, conv_b):
    b, l, w = hy.shape
    tm = 512
    nb8 = l // 8
    sds = jax.ShapeDtypeStruct((b, l, HYENA_WIDTH), F32)
    return pl.pallas_call(
        _hyena_pre_kernel,
        out_shape=(sds, sds),
        grid=(b, l // tm),
        in_specs=[pl.BlockSpec((1, tm, w), lambda bi, i: (bi, i, 0)),
                  pl.BlockSpec((1, 8, w), lambda bi, i: (bi, jnp.maximum(i * (tm // 8) - 1, 0), 0)),
                  pl.BlockSpec((1, 8, w), lambda bi, i: (bi, jnp.minimum((i + 1) * (tm // 8), nb8 - 1), 0)),
                  pl.BlockSpec((3, w), lambda bi, i: (0, 0)),
                  pl.BlockSpec((1, w), lambda bi, i: (0, 0))],
        out_specs=(pl.BlockSpec((1, tm, HYENA_WIDTH), lambda bi, i: (bi, i, 0)),
                   pl.BlockSpec((1, tm, HYENA_WIDTH), lambda bi, i: (bi, i, 0))),
        compiler_params=_cp(("parallel", "parallel"), VMEM_LIMIT),
        name="hyena_pre",
    )(hy, hy, hy, conv_w, conv_b.reshape(1, w))


def _filter_kernel(w1_ref, b1_ref, w2_ref, b2_ref, w3_ref, b3_ref, w4_ref, fq_ref,
                   k_ref, asum_ref, *, seq):
    i = pl.program_id(0)
    tr = k_ref.shape[0]
    c = HYENA_WIDTH
    m = i * tr + lax.broadcasted_iota(I32, (tr, 1), 0)
    t_idx = jnp.where(m < seq, m, 2 * seq - m)
    tf = t_idx.astype(F32)
    t = tf * (1.0 / (seq - 1))
    omega = (2.0 * math.pi / seq) * tf
    lane = lax.broadcasted_iota(I32, (tr, LANES), 1)
    band = (lane % FILTER_BANDS).astype(F32)
    fr = 1e-4 + band * ((FILTER_BANDS - 1 - 1e-4) / (FILTER_BANDS - 1))
    arg = fr * omega
    feat = jnp.where(lane < FILTER_BANDS, jnp.cos(arg),
                     jnp.where(lane < 2 * FILTER_BANDS, -jnp.sin(arg),
                               jnp.where(lane == 2 * FILTER_BANDS, t, 0.0)))
    fq = fq_ref[...]

    def layer(h, w_ref_, b_ref_):
        w_hi, w_lo = _split(w_ref_[...])
        return jnp.sin(fq * (_dot3r(h, w_hi, w_lo) + b_ref_[...]))

    h = layer(feat, w1_ref, b1_ref)
    h = layer(h, w2_ref, b2_ref)
    h = layer(h, w3_ref, b3_ref)
    w4_hi, w4_lo = _split(w4_ref[...])
    h = _dot3r(h, w4_hi, w4_lo)
    ch = lax.broadcasted_iota(I32, (1, c), 1).astype(F32)
    max_decay = math.log(HYENA_TARGET) / HYENA_FAST_DECAY
    min_decay = math.log(HYENA_TARGET) / HYENA_SLOW_DECAY
    delta = jnp.abs(min_decay + ch * ((max_decay - min_decay) / (c - 1)))
    decay = jnp.exp(-t * delta)
    k = jnp.where(m < seq, h[:, :c], jnp.where(m > seq, h[:, c:], 0.0)) * decay
    k_ref[...] = k

    @pl.when(i == 0)
    def _():
        asum_ref[...] = jnp.zeros_like(asum_ref)

    asum_ref[...] += jnp.sum(jnp.abs(k), axis=0, keepdims=True)


def hyena_filter_taps(seq, w1, b1, w2, b2, w3, b3, w4, fq):
    c = HYENA_WIDTH
    tr = 512
    fo = FILTER_ORDER
    w1p = jnp.concatenate([w1[1:1 + 2 * FILTER_BANDS], w1[0:1],
                           jnp.zeros((LANES - 2 * FILTER_BANDS - 1, fo), F32)], axis=0)
    full = lambda shape: pl.BlockSpec(shape, lambda i: (0, 0))
    return pl.pallas_call(
        functools.partial(_filter_kernel, seq=seq),
        out_shape=(jax.ShapeDtypeStruct((2 * seq, c), F32), jax.ShapeDtypeStruct((1, c), F32)),
        grid=(2 * seq // tr,),
        in_specs=[full((LANES, fo)), full((1, fo)), full((fo, fo)), full((1, fo)),
                  full((fo, fo)), full((1, fo)), full((fo, 2 * c)), full((1, fo))],
        out_specs=(pl.BlockSpec((tr, c), lambda i: (i, 0)), full((1, c))),
        compiler_params=_cp(("arbitrary",), VMEM_LIMIT),
        name="hyena_filter",
    )(w1p, b1.reshape(1, fo), w2, b2.reshape(1, fo), w3, b3.reshape(1, fo),
      w4, fq.reshape(1, fo))


def _dft_mats(n):
    k = np.arange(n)
    ang = -2.0 * np.pi * np.outer(k, k) / n
    return np.cos(ang), np.sin(ang)


def _split_np(m):
    m32 = jnp.asarray(m, F32)
    return _split(m32)


def _stage_outer_mats(n1):
    fr, fi = _dft_mats(n1)
    h = n1 // 2
    fwd_pair = np.block([[fr[:, :h], -fi[:, :h]], [fi[:, :h], fr[:, :h]]])
    fwd_real = np.concatenate([fr, fi], axis=0)
    inv_pair = np.block([[fr[:h, :], fi[:h, :]], [-fi[:h, :], fr[:h, :]]])
    return _split_np(fwd_pair), _split_np(fwd_real), _split_np(inv_pair)


def _stage_inner_mats(n2):
    fr, fi = _dft_mats(n2)
    fwd = np.block([[fr, -fi], [fi, fr]])
    inv = np.block([[fr, fi], [-fi, fr]])
    return _split_np(fwd), _split_np(inv)


def _left_dft_kernel(mh_ref, ml_ref, x_ref, o_ref):
    o_ref[0] = _dot3(mh_ref[...], ml_ref[...], x_ref[0])


def left_dft(mats, x, tc=1536):
    m_hi, m_lo = mats
    r, k = m_hi.shape
    p, _, cols = x.shape
    return pl.pallas_call(
        _left_dft_kernel,
        out_shape=jax.ShapeDtypeStruct((p, r, cols), F32),
        grid=(p, cols // tc),
        in_specs=[pl.BlockSpec((r, k), lambda pi, j: (0, 0)),
                  pl.BlockSpec((r, k), lambda pi, j: (0, 0)),
                  pl.BlockSpec((1, k, tc), lambda pi, j: (pi, 0, j))],
        out_specs=pl.BlockSpec((1, r, tc), lambda pi, j: (pi, 0, j)),
        compiler_params=_cp(("parallel", "parallel"), VMEM_LIMIT),
        name="hyena_dft_outer",
    )(m_hi, m_lo, x)


def _left_idft_kernel(mh_ref, ml_ref, y_ref, x0_ref, z_ref, d_ref, o_ref):
    conv = _dot3(mh_ref[...], ml_ref[...], y_ref[0])
    z = z_ref[0]
    o_ref[0] = x0_ref[0] * (conv + z * d_ref[...])


def left_idft_gate(mats, spec, x0, z, d_tiled, tc=1536):
    m_hi, m_lo = mats
    r, k = m_hi.shape
    p, _, cols = spec.shape
    return pl.pallas_call(
        _left_idft_kernel,
        out_shape=jax.ShapeDtypeStruct((p, r, cols), F32),
        grid=(p, cols // tc),
        in_specs=[pl.BlockSpec((r, k), lambda pi, j: (0, 0)),
                  pl.BlockSpec((r, k), lambda pi, j: (0, 0)),
                  pl.BlockSpec((1, k, tc), lambda pi, j: (pi, 0, j)),
                  pl.BlockSpec((1, r, tc), lambda pi, j: (pi, 0, j)),
                  pl.BlockSpec((1, r, tc), lambda pi, j: (pi, 0, j)),
                  pl.BlockSpec((1, tc), lambda pi, j: (0, j))],
        out_specs=pl.BlockSpec((1, r, tc), lambda pi, j: (pi, 0, j)),
        compiler_params=_cp(("parallel", "parallel"), VMEM_LIMIT),
        name="hyena_idft_outer",
    )(m_hi, m_lo, spec, x0, z, d_tiled)


def _twiddle(k1, n2, n):
    row = lax.broadcasted_iota(I32, (n2, LANES), 0)
    ang = ((row * k1) & (n - 1)).astype(F32) * (2.0 * math.pi / n)
    reps = HYENA_WIDTH // LANES
    c = jnp.concatenate([jnp.cos(ang)] * reps, axis=1)
    s = jnp.concatenate([jnp.sin(ang)] * reps, axis=1)
    return c, s


def _spec_filter_kernel(fh_ref, fl_ref, a_ref, scale_ref, o_ref, *, n):
    n2 = a_ref.shape[3]
    c, s = _twiddle(pl.program_id(0), n2, n)
    ar, ai = a_ref[0, 0, 0], a_ref[0, 1, 0]
    x = jnp.concatenate([ar * c + ai * s, ai * c - ar * s], axis=0)
    y = _dot3(fh_ref[...], fl_ref[...], x) * scale_ref[...]
    o_ref[0, 0] = y[:n2]
    o_ref[1, 0] = y[n2:]


def filter_spectrum(inner_fwd, a, scale, n):
    _, _, n1, n2, c = a.shape
    fh, fl = inner_fwd
    return pl.pallas_call(
        functools.partial(_spec_filter_kernel, n=n),
        out_shape=jax.ShapeDtypeStruct((2, n1, n2, c), F32),
        grid=(n1,),
        in_specs=[pl.BlockSpec((2 * n2, 2 * n2), lambda k: (0, 0)),
                  pl.BlockSpec((2 * n2, 2 * n2), lambda k: (0, 0)),
                  pl.BlockSpec((1, 2, 1, n2, c), lambda k: (0, 0, k, 0, 0)),
                  pl.BlockSpec((1, c), lambda k: (0, 0))],
        out_specs=pl.BlockSpec((2, 1, n2, c), lambda k: (0, k, 0, 0)),
        compiler_params=_cp(("parallel",), VMEM_LIMIT),
        name="hyena_filter_spectrum",
    )(fh, fl, a, scale)


def _spec_conv_kernel(fh_ref, fl_ref, ih_ref, il_ref, a_ref, k_ref, o_ref, *, n):
    n2 = a_ref.shape[3]
    c, s = _twiddle(pl.program_id(1), n2, n)
    ar, ai = a_ref[0, 0, 0], a_ref[0, 1, 0]
    x = jnp.concatenate([ar * c + ai * s, ai * c - ar * s], axis=0)
    y = _dot3(fh_ref[...], fl_ref[...], x)
    yr, yi = y[:n2], y[n2:]
    kr, ki = k_ref[0, 0], k_ref[1, 0]
    p = jnp.concatenate([yr * kr - yi * ki, yr * ki + yi * kr], axis=0)
    w = _dot3(ih_ref[...], il_ref[...], p)
    wr, wi = w[:n2], w[n2:]
    o_ref[0, 0, 0] = wr * c - wi * s
    o_ref[0, 1, 0] = wr * s + wi * c


def spectrum_convolve(inner, a, kspec, n):
    p, _, n1, n2, c = a.shape
    (fh, fl), (ih, il) = inner
    msp = pl.BlockSpec((2 * n2, 2 * n2), lambda pi, k: (0, 0))
    return pl.pallas_call(
        functools.partial(_spec_conv_kernel, n=n),
        out_shape=jax.ShapeDtypeStruct((p, 2, n1, n2, c), F32),
        grid=(p, n1),
        in_specs=[msp, msp, msp, msp,
                  pl.BlockSpec((1, 2, 1, n2, c), lambda pi, k: (pi, 0, k, 0, 0)),
                  pl.BlockSpec((2, 1, n2, c), lambda pi, k: (0, k, 0, 0))],
        out_specs=pl.BlockSpec((1, 2, 1, n2, c), lambda pi, k: (pi, 0, k, 0, 0)),
        compiler_params=_cp(("parallel", "parallel"), VMEM_LIMIT),
        name="hyena_spectrum_convolve",
    )(fh, fl, ih, il, a, kspec)


def hyena_filter_spectrum(seq, p):
    n = 2 * seq
    n1, n2 = _fft_factors(n)
    c = HYENA_WIDTH
    taps, asum = hyena_filter_taps(seq, p['filt_w1'], p['filt_b1'], p['filt_w2'], p['filt_b2'],
                                   p['filt_w3'], p['filt_b3'], p['filt_w4'], p['filt_freq'])
    _, fwd_real, _ = _stage_outer_mats(n1)
    a = left_dft(fwd_real, taps.reshape(1, n1, n2 * c))
    inner_fwd, _ = _stage_inner_mats(n2)
    scale = 1.0 / (asum * n)
    return filter_spectrum(inner_fwd, a.reshape(1, 2, n1, n2, c), scale, n)


def hyena_long_conv(x0, z, kspec, hy_d):
    b, l, c = z.shape
    n = 2 * l
    n1, n2 = _fft_factors(n)
    fwd_pair, _, inv_pair = _stage_outer_mats(n1)
    inner = _stage_inner_mats(n2)
    zp = z.reshape(b // 2, n1, n2 * c)
    a = left_dft(fwd_pair, zp)
    w = spectrum_convolve(inner, a.reshape(b // 2, 2, n1, n2, c), kspec, n)
    d_tiled = jnp.tile(hy_d.reshape(1, c), (1, n2))
    y = left_idft_gate(inv_pair, w.reshape(b // 2, 2 * n1, n2 * c),
                       x0.reshape(b // 2, n1, n2 * c), zp, d_tiled)
    return y.reshape(b, l, c)


def _rope_table_kernel(inv_ref, cos_ref, sin_ref):
    tm = cos_ref.shape[0]
    pos = (pl.program_id(0) * tm + lax.broadcasted_iota(I32, (tm, LANES), 0)).astype(F32)
    ang = pos * inv_ref[...]
    lane = lax.broadcasted_iota(I32, (tm, LANES), 1)
    cos_ref[...] = jnp.cos(ang)
    sin_ref[...] = jnp.where(lane % HEAD_DIM < HEAD_DIM // 2, -jnp.sin(ang), jnp.sin(ang))


def rope_tables(seq):
    inv = ROPE_THETA ** (-jnp.arange(0, HEAD_DIM, 2, dtype=F32) / HEAD_DIM)
    inv128 = jnp.tile(inv, LANES // (HEAD_DIM // 2)).reshape(1, LANES)
    tm = 512
    sds = jax.ShapeDtypeStruct((seq, LANES), F32)
    return pl.pallas_call(
        _rope_table_kernel,
        out_shape=(sds, sds),
        grid=(seq // tm,),
        in_specs=[pl.BlockSpec((1, LANES), lambda i: (0, 0))],
        out_specs=(pl.BlockSpec((tm, LANES), lambda i: (i, 0)), pl.BlockSpec((tm, LANES), lambda i: (i, 0))),
        compiler_params=_cp(("parallel",), VMEM_LIMIT),
        name="rope_tables",
    )(inv128)


def _norm_rope(t, g, cos, sin_signed, ones_bd, scale):
    sq = t * t
    parts = []
    for c0 in range(0, ATTN_WIDTH, 256):
        hi, lo = _split(sq[:, c0:c0 + 256])
        parts.append(_dot(hi, ones_bd) + _dot(lo, ones_bd))
    ms = jnp.concatenate(parts, axis=1) * (1.0 / HEAD_DIM)
    tn = t * lax.rsqrt(ms + EPS) * g
    lane = lax.broadcasted_iota(I32, (t.shape[0], LANES), 1)
    first_half = lane % HEAD_DIM < HEAD_DIM // 2
    outs = []
    for c0 in range(0, ATTN_WIDTH, LANES):
        x = tn[:, c0:c0 + LANES]
        rot = jnp.where(first_half, pltpu.roll(x, LANES - HEAD_DIM // 2, 1), pltpu.roll(x, HEAD_DIM // 2, 1))
        outs.append((x * cos + rot * sin_signed) * scale)
    return jnp.concatenate(outs, axis=1)


def _qk_prep_kernel(qkv_ref, cos_ref, sin_ref, qg_ref, kg_ref, bd_ref, q_ref, k_ref, v_ref):
    w = ATTN_WIDTH
    cos, sin = cos_ref[...], sin_ref[...]
    bd = bd_ref[...]
    q_ref[0] = _norm_rope(qkv_ref[0, :, :w], qg_ref[...], cos, sin, bd, HEAD_DIM ** -0.5).astype(BF16)
    k_ref[0] = _norm_rope(qkv_ref[0, :, w:2 * w], kg_ref[...], cos, sin, bd, 1.0).astype(BF16)
    v_ref[0] = qkv_ref[0, :, 2 * w:].astype(BF16)


def qk_prep(qkv, cos, sin, q_g, k_g):
    b, l, _ = qkv.shape
    tm = 512
    w = ATTN_WIDTH
    head = np.arange(256) // HEAD_DIM
    ones_bd = jnp.asarray(head[:, None] == head[None, :], BF16)
    sds = jax.ShapeDtypeStruct((b, l, w), BF16)
    tile_g = lambda g: jnp.tile(g, w // HEAD_DIM).reshape(1, w)
    ospec = pl.BlockSpec((1, tm, w), lambda bi, i: (bi, i, 0))
    return pl.pallas_call(
        _qk_prep_kernel,
        out_shape=(sds, sds, sds),
        grid=(b, l // tm),
        in_specs=[pl.BlockSpec((1, tm, 3 * w), lambda bi, i: (bi, i, 0)),
                  pl.BlockSpec((tm, LANES), lambda bi, i: (i, 0)),
                  pl.BlockSpec((tm, LANES), lambda bi, i: (i, 0)),
                  pl.BlockSpec((1, w), lambda bi, i: (0, 0)),
                  pl.BlockSpec((1, w), lambda bi, i: (0, 0)),
                  pl.BlockSpec((256, 256), lambda bi, i: (0, 0))],
        out_specs=(ospec, ospec, ospec),
        compiler_params=_cp(("parallel", "parallel"), VMEM_LIMIT),
        name="qk_norm_rope",
    )(qkv, cos, sin, tile_g(q_g), tile_g(k_g), ones_bd)


def _band_attn_kernel(q_ref, kp_ref, kc_ref, kn_ref, vp_ref, vc_ref, vn_ref, o_ref, lse_ref, *, m_len):
    i = pl.program_id(2)
    q = q_ref[0]
    tq = q.shape[0]
    h = ATTN_HALF
    kk = jnp.concatenate([kp_ref[0], kc_ref[0], kn_ref[0]], axis=0)
    vv = jnp.concatenate([vp_ref[0], vc_ref[0], vn_ref[0]], axis=0)
    qi = lax.broadcasted_iota(I32, (tq, tq + 2 * h), 0)
    kj = lax.broadcasted_iota(I32, (tq, tq + 2 * h), 1)
    rel = kj - h - qi
    kpos = i * tq - h + kj
    mask = (jnp.abs(rel) <= h) & (kpos >= 0) & (kpos < m_len)
    outs, lses = [], []
    for hd in range(HEADS_PER_GROUP):
        sl = slice(hd * HEAD_DIM, (hd + 1) * HEAD_DIM)
        s = lax.dot_general(q[:, sl], kk[:, sl], (((1,), (1,)), ((), ())), preferred_element_type=F32)
        s = jnp.where(mask, s, NEG_BIG)
        mx = jnp.max(s, axis=1, keepdims=True)
        p = jnp.exp(s - mx)
        den = jnp.sum(p, axis=1, keepdims=True)
        o = _dot(p.astype(BF16), vv[:, sl]) / den
        outs.append(o)
        lses.append(jnp.broadcast_to(mx + jnp.log(den), (tq, HEAD_DIM)))
    o_ref[0] = jnp.concatenate(outs, axis=1)
    lse_ref[0] = jnp.concatenate(lses, axis=1)


def band_attention(q, k, v, group, dil):
    b, l, w = q.shape
    m_len = l // dil
    tq = min(128, m_len)
    hb = ATTN_HALF
    gw = ATTN_OUT
    nhb = m_len // hb
    view = lambda t: t.reshape(b, m_len, dil * w)
    col = lambda r: r * (w // gw) + group
    cur = pl.BlockSpec((1, tq, gw), lambda bi, r, i: (bi, i, col(r)))
    prev = pl.BlockSpec((1, hb, gw), lambda bi, r, i: (bi, jnp.maximum(i * (tq // hb) - 1, 0), col(r)))
    nxt = pl.BlockSpec((1, hb, gw), lambda bi, r, i: (bi, jnp.minimum((i + 1) * (tq // hb), nhb - 1), col(r)))
    ospec = pl.BlockSpec((1, tq, gw), lambda bi, r, i: (bi, i, r))
    sds = jax.ShapeDtypeStruct((b, m_len, dil * gw), F32)
    qv, kv, vv = view(q), view(k), view(v)
    o, lse = pl.pallas_call(
        functools.partial(_band_attn_kernel, m_len=m_len),
        out_shape=(sds, sds),
        grid=(b, dil, m_len // tq),
        in_specs=[cur, prev, cur, nxt, prev, cur, nxt],
        out_specs=(ospec, ospec),
        compiler_params=_cp(("parallel", "parallel", "parallel"), VMEM_LIMIT),
        name=f"band_attention_d{dil}",
    )(qv, kv, kv, kv, vv, vv, vv)
    return o.reshape(b, l, gw), lse.reshape(b, l, gw)


def _pooled(a_cur, a_prev, a_next, i, last, seq):
    tm = a_cur.shape[0]
    hl = POOL_HALO
    prev = jnp.where(i > 0, a_prev, 0.0)
    nxt = jnp.where(i < last, a_next, 0.0)
    ext = jnp.concatenate([prev, a_cur, nxt], axis=0)
    rows = ext.shape[0]
    lane = lax.broadcasted_iota(I32, (1, POOL_WIDTH), 1)
    pos = i * tm + lax.broadcasted_iota(I32, (tm, 1), 0)
    trail = ext
    win = None
    count = None
    for gi, w in enumerate(POOL_WINDOWS):
        trail = trail + pltpu.roll(trail, w // 2, 0)
        left, right = w // 2, w - 1 - w // 2
        centred = pltpu.roll(trail, rows - right, 0) if right else trail
        centred = centred[hl:hl + tm]
        cnt = (jnp.minimum(pos + right, seq - 1) - jnp.maximum(pos - left, 0) + 1).astype(F32)
        if win is None:
            win, count = centred, jnp.broadcast_to(cnt, (tm, POOL_WIDTH))
        else:
            sel = lane >= gi * POOL_GROUP
            win = jnp.where(sel, centred, win)
            count = jnp.where(sel, cnt, count)
    return win / count - a_cur


def _merge_kernel(a_ref, ap_ref, an_ref, yb_ref, o0_ref, o1_ref, o2_ref, l0_ref, l1_ref, l2_ref,
                  gt_ref, x_ref, mod_ref, pw_ref, ps_ref, wa_ref, wb_ref, wc_ref, wo_ref, out_ref, *, seq):
    i = pl.program_id(1)
    last = pl.num_programs(1) - 1
    d = D_MODEL
    pooled = _pooled(a_ref[0], ap_ref[0], an_ref[0], i, last, seq)
    ya = _dot(pooled.astype(BF16), pw_ref[...]) * ps_ref[...]
    l0, l1, l2 = l0_ref[0], l1_ref[0], l2_ref[0]
    mx = jnp.maximum(jnp.maximum(l0, l1), l2)
    e0, e1, e2 = jnp.exp(l0 - mx), jnp.exp(l1 - mx), jnp.exp(l2 - mx)
    yc = (e0 * o0_ref[0] + e1 * o1_ref[0] + e2 * o2_ref[0]) / (e0 + e1 + e2)
    merged = (jax.nn.sigmoid(gt_ref[0, :, :d]) * _dot(ya.astype(BF16), wa_ref[...])
              + jax.nn.sigmoid(gt_ref[0, :, d:2 * d]) * _dot(yb_ref[0].astype(BF16), wb_ref[...])
              + jax.nn.sigmoid(gt_ref[0, :, 2 * d:]) * _dot(yc.astype(BF16), wc_ref[...]))
    out_ref[0] = x_ref[0] + mod_ref[0, 2:3, :] * _dot(merged.astype(BF16), wo_ref[...])


def merge_branches(a, yb, attn, gates, x, mod, pool_bd, pool_scale, wa, wb, wc, wo):
    b, l, d = x.shape
    tm = 256
    hl = POOL_HALO
    nh = l // hl
    (o0, l0), (o1, l1), (o2, l2) = attn
    row = lambda w: pl.BlockSpec((1, tm, w), lambda bi, i: (bi, i, 0))
    full = lambda s: pl.BlockSpec(s, lambda bi, i: (0, 0))
    return pl.pallas_call(
        functools.partial(_merge_kernel, seq=l),
        out_shape=jax.ShapeDtypeStruct((b, l, d), F32),
        grid=(b, l // tm),
        in_specs=[row(POOL_WIDTH),
                  pl.BlockSpec((1, hl, POOL_WIDTH), lambda bi, i: (bi, jnp.maximum(i * (tm // hl) - 1, 0), 0)),
                  pl.BlockSpec((1, hl, POOL_WIDTH), lambda bi, i: (bi, jnp.minimum((i + 1) * (tm // hl), nh - 1), 0)),
                  row(HYENA_WIDTH), row(ATTN_OUT), row(ATTN_OUT), row(ATTN_OUT),
                  row(ATTN_OUT), row(ATTN_OUT), row(ATTN_OUT), row(3 * d), row(d),
                  pl.BlockSpec((1, N_MOD, d), lambda bi, i: (bi, 0, 0)),
                  full((POOL_WIDTH, POOL_WIDTH)), full((1, POOL_WIDTH)), full((POOL_WIDTH, d)),
                  full((HYENA_WIDTH, d)), full((ATTN_OUT, d)), full((d, d))],
        out_specs=row(d),
        compiler_params=_cp(("parallel", "parallel"), VMEM_LIMIT),
        name="merge_branches",
    )(a, a, a, yb, o0, o1, o2, l0, l1, l2, gates, x, mod, pool_bd, pool_scale, wa, wb, wc, wo)


def _router_kernel(x_ref, mod_ref, g_ref, wr_ref, br_ref, tri_ref, h_ref, idx_ref, gw_ref, rank_ref,
                   cnt_ref, carry_ref):
    first = (pl.program_id(0) == 0) & (pl.program_id(1) == 0)

    @pl.when(first)
    def _():
        carry_ref[...] = jnp.zeros_like(carry_ref)

    h = _modulated_norm(x_ref[0], g_ref[...], mod_ref[0, 4:5, :], mod_ref[0, 3:4, :])
    h_ref[0] = h
    w_hi, w_lo = _split(wr_ref[...])
    logits = _dot3r(h, w_hi, w_lo) + br_ref[...]
    tm, ne = logits.shape
    lane = lax.broadcasted_iota(I32, (tm, ne), 1).astype(F32)
    lane_out = lax.broadcasted_iota(I32, (tm, LANES), 1)
    work = logits
    vals, idxs = [], []
    member = jnp.zeros((tm, ne), F32)
    for _ in range(TOP_K):
        mx = jnp.max(work, axis=1, keepdims=True)
        idx = jnp.min(jnp.where(work == mx, lane, float(ne)), axis=1, keepdims=True)
        hit = lane == idx
        member = jnp.where(hit, 1.0, member)
        work = jnp.where(hit, -jnp.inf, work)
        vals.append(mx)
        idxs.append(idx)
    exps = [jnp.exp(v - vals[0]) for v in vals]
    den = exps[0] + exps[1] + exps[2] + exps[3]
    prefix = _dot(tri_ref[...], member.astype(BF16)) + carry_ref[...]
    idx_out = jnp.zeros((tm, LANES), I32)
    gw_out = jnp.zeros((tm, LANES), F32)
    rank_out = jnp.zeros((tm, LANES), I32)
    for k in range(TOP_K):
        rk = jnp.sum(jnp.where(lane == idxs[k], prefix, 0.0), axis=1, keepdims=True)
        idx_out = jnp.where(lane_out == k, idxs[k].astype(I32), idx_out)
        gw_out = jnp.where(lane_out == k, exps[k] / den, gw_out)
        rank_out = jnp.where(lane_out == k, rk.astype(I32), rank_out)
    idx_ref[0] = idx_out
    gw_ref[0] = gw_out
    rank_ref[0] = rank_out
    carry_ref[...] += jnp.sum(member, axis=0, keepdims=True)
    cnt_ref[...] = carry_ref[...]


def moe_router(x, mod, g2, w_router, b_router):
    b, l, d = x.shape
    tm = 256
    ne = N_EXPERTS
    tri = jnp.asarray(np.tril(np.ones((tm, tm), np.float32), -1), BF16)
    row = lambda w: pl.BlockSpec((1, tm, w), lambda bi, i: (bi, i, 0))
    full = lambda s: pl.BlockSpec(s, lambda bi, i: (0, 0))
    return pl.pallas_call(
        _router_kernel,
        out_shape=(jax.ShapeDtypeStruct((b, l, d), F32),
                   jax.ShapeDtypeStruct((b, l, LANES), I32),
                   jax.ShapeDtypeStruct((b, l, LANES), F32),
                   jax.ShapeDtypeStruct((b, l, LANES), I32),
                   jax.ShapeDtypeStruct((1, ne), F32)),
        grid=(b, l // tm),
        in_specs=[row(d), pl.BlockSpec((1, N_MOD, d), lambda bi, i: (bi, 0, 0)), full((1, d)),
                  full((d, ne)), full((1, ne)), full((tm, tm))],
        out_specs=(row(d), row(LANES), row(LANES), row(LANES), full((1, ne))),
        scratch_shapes=[pltpu.VMEM((1, ne), F32)],
        compiler_params=_cp(("arbitrary", "arbitrary"), VMEM_LIMIT),
        name="moe_router",
    )(x, mod, g2.reshape(1, d), w_router, b_router.reshape(1, ne), tri)


def _gather_rows(src_hbm, idx_ref, buf, sem, nrows, do_wait):
    def body(r, carry):
        cp = pltpu.make_async_copy(src_hbm.at[pl.ds(idx_ref[0, 0, r], 1), :], buf.at[pl.ds(r, 1), :], sem)
        if do_wait:
            cp.wait()
        else:
            cp.start()
        return carry
    lax.fori_loop(0, nrows, body, 0)


def _expert_kernel(meta_ref, tok_ref, tokn_ref, h_hbm, w1_ref, b1_ref, w2_ref, b2_ref, o_ref, buf, sems):
    i = pl.program_id(0)
    nblk = pl.num_programs(0)
    used = meta_ref[0]
    bm = buf.shape[1]
    slot = i % 2

    @pl.when(i == 0)
    def _():
        _gather_rows(h_hbm, tok_ref, buf.at[0], sems.at[0], bm, False)

    @pl.when(i + 1 < nblk)
    def _():
        _gather_rows(h_hbm, tokn_ref, buf.at[1 - slot], sems.at[1 - slot], bm, False)

    _gather_rows(h_hbm, tok_ref, buf.at[slot], sems.at[slot], bm, True)

    @pl.when(i < used)
    def _():
        f = D_MODEL
        xb = buf[slot].astype(BF16)
        u = _dot(xb, w1_ref[0]) + b1_ref[0]
        x_glu = jnp.minimum(u[:, :f], SWIGLU_LIMIT)
        x_lin = jnp.clip(u[:, f:], -SWIGLU_LIMIT, SWIGLU_LIMIT)
        act = x_glu * jax.nn.sigmoid(SWIGLU_ALPHA * x_glu) * (x_lin + 1.0)
        o_ref[...] = _dot(act.astype(BF16), w2_ref[0]) + b2_ref[0]

    @pl.when(i >= used)
    def _():
        o_ref[...] = jnp.zeros_like(o_ref)


def moe_experts(h_flat, row_tok, blk_e, n_used, w1_bf, b1, w2_bf, b2):
    t, d = h_flat.shape
    nblk, _, bm = row_tok.shape
    ne, _, f2 = w1_bf.shape
    meta = jnp.concatenate([n_used.reshape(1).astype(I32), blk_e.astype(I32)])
    grid_spec = pltpu.PrefetchScalarGridSpec(
        num_scalar_prefetch=1,
        grid=(nblk,),
        in_specs=[pl.BlockSpec((1, 1, bm), lambda i, m: (i, 0, 0), memory_space=pltpu.SMEM),
                  pl.BlockSpec((1, 1, bm), lambda i, m: (jnp.minimum(i + 1, nblk - 1), 0, 0),
                               memory_space=pltpu.SMEM),
                  pl.BlockSpec(memory_space=pl.ANY),
                  pl.BlockSpec((1, d, f2), lambda i, m: (m[i + 1], 0, 0)),
                  pl.BlockSpec((1, 1, f2), lambda i, m: (m[i + 1], 0, 0)),
                  pl.BlockSpec((1, f2 // 2, d), lambda i, m: (m[i + 1], 0, 0)),
                  pl.BlockSpec((1, 1, d), lambda i, m: (m[i + 1], 0, 0))],
        out_specs=pl.BlockSpec((bm, d), lambda i, m: (i, 0)),
        scratch_shapes=[pltpu.VMEM((2, bm, d), F32), pltpu.SemaphoreType.DMA((2,))],
    )
    return pl.pallas_call(
        _expert_kernel,
        out_shape=jax.ShapeDtypeStruct((nblk * bm, d), F32),
        grid_spec=grid_spec,
        compiler_params=_cp(("arbitrary",), VMEM_LIMIT),
        name="moe_experts",
    )(meta, row_tok, row_tok, h_flat, w1_bf, b1.reshape(ne, 1, f2), w2_bf, b2.reshape(ne, 1, d))


def _combine_kernel(dst_ref, dstn_ref, eo_hbm, x_ref, gw_ref, mod_ref, o_ref, buf, sems):
    i = pl.program_id(0)
    nt = pl.num_programs(0)
    tm = x_ref.shape[0]
    slot = i % 2
    nrows = TOP_K * tm

    @pl.when(i == 0)
    def _():
        _gather_rows(eo_hbm, dst_ref, buf.at[0], sems.at[0], nrows, False)

    @pl.when(i + 1 < nt)
    def _():
        _gather_rows(eo_hbm, dstn_ref, buf.at[1 - slot], sems.at[1 - slot], nrows, False)

    _gather_rows(eo_hbm, dst_ref, buf.at[slot], sems.at[slot], nrows, True)
    gw = gw_ref[...]
    acc = jnp.zeros((tm, D_MODEL), F32)
    for k in range(TOP_K):
        acc = acc + gw[:, k:k + 1] * buf[slot, k * tm:(k + 1) * tm, :]
    o_ref[...] = x_ref[...] + mod_ref[0] * acc


def moe_combine(x_flat, expert_out, dest, gate_w, gate2_rows):
    t, d = x_flat.shape
    nt, _, n4 = dest.shape
    tm = n4 // TOP_K
    return pl.pallas_call(
        _combine_kernel,
        out_shape=jax.ShapeDtypeStruct((t, d), F32),
        grid=(nt,),
        in_specs=[pl.BlockSpec((1, 1, n4), lambda i: (i, 0, 0), memory_space=pltpu.SMEM),
                  pl.BlockSpec((1, 1, n4), lambda i: (jnp.minimum(i + 1, nt - 1), 0, 0),
                               memory_space=pltpu.SMEM),
                  pl.BlockSpec(memory_space=pl.ANY),
                  pl.BlockSpec((tm, d), lambda i: (i, 0)),
                  pl.BlockSpec((tm, LANES), lambda i: (i, 0)),
                  pl.BlockSpec((1, 1, d), lambda i: (i, 0, 0))],
        out_specs=pl.BlockSpec((tm, d), lambda i: (i, 0)),
        scratch_shapes=[pltpu.VMEM((2, n4, d), F32), pltpu.SemaphoreType.DMA((2,))],
        compiler_params=_cp(("arbitrary",), VMEM_LIMIT),
        name="moe_combine",
    )(dest, dest, expert_out, x_flat, gate_w, gate2_rows)


def moe_block(x, mod, p):
    b, l, d = x.shape
    t = b * l
    bm = MOE_ROWS
    tmc = 128
    h, top_i, gate_w, rank, counts = moe_router(x, mod, p['norm2_g'], p['w_router'], p['b_router'])
    top_i = top_i.reshape(t, LANES)[:, :TOP_K]
    rank = rank.reshape(t, LANES)[:, :TOP_K]
    counts = counts.reshape(N_EXPERTS).astype(I32)
    pcounts = (counts + bm - 1) // bm * bm
    pends = jnp.cumsum(pcounts)
    pstarts = pends - pcounts
    dest = pstarts[top_i] + rank
    nblk = (t * TOP_K) // bm + N_EXPERTS
    tok = jnp.broadcast_to(jnp.arange(t, dtype=I32)[:, None], (t, TOP_K))
    row_tok = jnp.zeros((nblk * bm,), I32).at[dest.reshape(-1)].set(tok.reshape(-1))
    blk_e = jnp.minimum(jnp.searchsorted(pends, jnp.arange(nblk, dtype=I32) * bm, side='right'),
                        N_EXPERTS - 1)
    n_used = pends[-1] // bm
    eo = moe_experts(h.reshape(t, d), row_tok.reshape(nblk, 1, bm), blk_e, n_used,
                     p['w_mlp1'], p['b_mlp1'], p['w_mlp2'], p['b_mlp2'])
    dest_tiles = dest.reshape(t // tmc, tmc, TOP_K).transpose(0, 2, 1).reshape(t // tmc, 1, TOP_K * tmc)
    gate2 = jnp.repeat(mod[:, 5:6, :], l // tmc, axis=0)
    y = moe_combine(x.reshape(t, d), eo, dest_tiles, gate_w.reshape(t, LANES), gate2)
    return y.reshape(b, l, d)


def token_mixer_block(x, mod, p, cos, sin, kspec):
    a, hy, qkv, gates = in_projection(x, mod, p['norm1_g'], p['w_in'], p['b_in'])
    x0, z = hyena_pre(hy, p['hy_conv_w'], p['hy_conv_b'])
    yb = hyena_long_conv(x0, z, kspec, p['hy_d'])
    q, k, v = qk_prep(qkv, cos, sin, p['q_norm_g'], p['k_norm_g'])
    attn = [band_attention(q, k, v, gi, dil) for gi, (_, dil) in enumerate(ATTN_GROUPS)]
    return merge_branches(a, yb, attn, gates, x, mod, p['pool_bd'], p['pool_scale'],
                          p['w_branch_a'], p['w_branch_b'], p['w_branch_c'], p['w_o'])


def _pool_block_diag(pool_w):
    g = POOL_GROUP
    bd = jnp.zeros((POOL_WIDTH, POOL_WIDTH), F32)
    for gi in range(len(POOL_WINDOWS)):
        bd = bd.at[gi * g:(gi + 1) * g, gi * g:(gi + 1) * g].set(pool_w[gi])
    return bd.astype(BF16)


def kernel(x_prompt, x_sample, c_prompt, c_sample, ada_w, ada_b, norm1_g, w_in, b_in, pool_w, pool_scale, hy_conv_w, hy_conv_b, filt_w1, filt_b1, filt_w2, filt_b2, filt_w3, filt_b3, filt_w4, filt_freq, hy_d, q_norm_g, k_norm_g, w_branch_a, w_branch_b, w_branch_c, w_o, norm2_g, w_router, b_router, w_mlp1, b_mlp1, w_mlp2, b_mlp2):
    depth = ada_w.shape[0]
    d = D_MODEL
    groups = [(x_prompt, c_prompt), (x_sample, c_sample)]
    nb = [x.shape[0] for x, _ in groups]
    c_all = jnp.concatenate([c for _, c in groups], axis=0)
    pad = (-c_all.shape[0]) % 8
    c_all = jnp.pad(c_all, ((0, pad), (0, 0)))
    mod_all = ada_modulation(c_all, ada_w, ada_b)
    tables = {}
    for x, _ in groups:
        l = x.shape[1]
        if l not in tables:
            tables[l] = rope_tables(l)
    ys = [x for x, _ in groups]
    for layer in range(depth):
        p = {
            'norm1_g': norm1_g[layer], 'w_in': w_in[layer].astype(BF16), 'b_in': b_in[layer],
            'pool_bd': _pool_block_diag(pool_w[layer]), 'pool_scale': pool_scale[layer].reshape(1, POOL_WIDTH),
            'hy_conv_w': hy_conv_w[layer], 'hy_conv_b': hy_conv_b[layer],
            'filt_w1': filt_w1[layer], 'filt_b1': filt_b1[layer], 'filt_w2': filt_w2[layer],
            'filt_b2': filt_b2[layer], 'filt_w3': filt_w3[layer], 'filt_b3': filt_b3[layer],
            'filt_w4': filt_w4[layer], 'filt_freq': filt_freq[layer], 'hy_d': hy_d[layer],
            'q_norm_g': q_norm_g[layer], 'k_norm_g': k_norm_g[layer],
            'w_branch_a': w_branch_a[layer].astype(BF16), 'w_branch_b': w_branch_b[layer].astype(BF16),
            'w_branch_c': w_branch_c[layer].astype(BF16), 'w_o': w_o[layer].astype(BF16),
            'norm2_g': norm2_g[layer], 'w_router': w_router[layer], 'b_router': b_router[layer],
            'w_mlp1': w_mlp1[layer].astype(BF16), 'b_mlp1': b_mlp1[layer],
            'w_mlp2': w_mlp2[layer].astype(BF16), 'b_mlp2': b_mlp2[layer],
        }
        kspecs = {}
        off = 0
        for gi, (x, _) in enumerate(groups):
            l = x.shape[1]
            if l not in kspecs:
                kspecs[l] = hyena_filter_spectrum(l, p)
            mod = mod_all[layer, off:off + nb[gi]].reshape(nb[gi], N_MOD, d)
            off += nb[gi]
            cos, sin = tables[l]
            x1 = token_mixer_block(ys[gi], mod, p, cos, sin, kspecs[l])
            ys[gi] = moe_block(x1, mod, p)
    return tuple(ys)
```

```python
import functools
import math

import numpy as np
import jax
import jax.numpy as jnp
from jax import lax
from jax.experimental import pallas as pl
from jax.experimental.pallas import tpu as pltpu

F32 = jnp.float32
BF16 = jnp.bfloat16
I32 = jnp.int32

D_MODEL = 1024
N_MOD = 6
EPS = 1e-6
POOL_WINDOWS = (2, 4, 8, 16)
POOL_GROUP = 96
POOL_WIDTH = 384
POOL_HALO = 16
HYENA_WIDTH = 384
FILTER_ORDER = 64
FILTER_BANDS = 16
HYENA_FAST_DECAY = 0.3
HYENA_SLOW_DECAY = 1.5
HYENA_TARGET = 1e-2
HEAD_DIM = 64
ATTN_GROUPS = ((128, 1), (512, 4), (2048, 16))
HEADS_PER_GROUP = 4
ATTN_WIDTH = 768
ATTN_OUT = 256
ATTN_HALF = 64
ATTN_QUERY_TILE = 128
ATTN_STEP_ROWS = 512
ROPE_THETA = 10000.0
NEG_BIG = -1e30
OFF_HYENA = POOL_WIDTH
OFF_ATTN = OFF_HYENA + 3 * HYENA_WIDTH
OFF_GATE = OFF_ATTN + 3 * ATTN_WIDTH
IN_COLS = OFF_GATE + 3 * D_MODEL
N_EXPERTS = 32
TOP_K = 4
SWIGLU_LIMIT = 7.0
SWIGLU_ALPHA = 1.702
LANES = 128
MOE_ROWS = 256
VMEM_LIMIT = 56 * 1024 * 1024


def _cp(sem, vmem=None):
    return pltpu.CompilerParams(dimension_semantics=sem, vmem_limit_bytes=vmem)


def _split(x):
    hi = x.astype(BF16)
    lo = (x - hi.astype(F32)).astype(BF16)
    return hi, lo


def _dot(a, b):
    return jnp.dot(a, b, preferred_element_type=F32)


def _dot3(a_hi, a_lo, b):
    b_hi, b_lo = _split(b)
    return _dot(a_hi, b_hi) + _dot(a_hi, b_lo) + _dot(a_lo, b_hi)


def _dot3r(a, b_hi, b_lo):
    a_hi, a_lo = _split(a)
    return _dot(a_hi, b_hi) + _dot(a_lo, b_hi) + _dot(a_hi, b_lo)


def _fft_factors(n):
    n2 = 128 if n >= 16384 else 64
    return n // n2, n2


def _ada_kernel(c_ref, w_ref, b_ref, o_ref):
    c = c_ref[...]
    s = c * jax.nn.sigmoid(c)
    w_hi, w_lo = _split(w_ref[0])
    o_ref[0] = _dot3r(s, w_hi, w_lo) + b_ref[0]


def ada_modulation(c_all, ada_w, ada_b):
    nl, d, n = ada_w.shape
    bp = c_all.shape[0]
    tn = 1536
    return pl.pallas_call(
        _ada_kernel,
        out_shape=jax.ShapeDtypeStruct((nl, bp, n), F32),
        grid=(nl, n // tn),
        in_specs=[pl.BlockSpec((bp, d), lambda l, j: (0, 0)),
                  pl.BlockSpec((1, d, tn), lambda l, j: (l, 0, j)),
                  pl.BlockSpec((1, 1, tn), lambda l, j: (l, 0, j))],
        out_specs=pl.BlockSpec((1, bp, tn), lambda l, j: (l, 0, j)),
        compiler_params=_cp(("parallel", "parallel"), VMEM_LIMIT),
        name="ada_modulation",
    )(c_all, ada_w, ada_b.reshape(nl, 1, n))


IN_SPLITS = ((0, POOL_WIDTH), (OFF_HYENA, 3 * HYENA_WIDTH), (OFF_ATTN, 3 * ATTN_WIDTH),
             (OFF_GATE, 3 * D_MODEL))
IN_CHUNK = 768


def _modulated_norm(x, g, scale, shift):
    ms = jnp.mean(x * x, axis=-1, keepdims=True)
    return (x * lax.rsqrt(ms + EPS) * g) * (1.0 + scale) + shift


def _inproj_kernel(x_ref, mod_ref, g_ref, w_ref, b_ref, a_ref, hy_ref, qkv_ref, gt_ref):
    h = _modulated_norm(x_ref[0], g_ref[...], mod_ref[0, 1:2, :], mod_ref[0, 0:1, :])
    hb = h.astype(BF16)
    for (off, width), o_ref in zip(IN_SPLITS, (a_ref, hy_ref, qkv_ref, gt_ref)):
        for c0 in range(0, width, IN_CHUNK):
            c1 = min(c0 + IN_CHUNK, width)
            o_ref[0, :, c0:c1] = (_dot(hb, w_ref[:, off + c0:off + c1])
                                  + b_ref[:, off + c0:off + c1])


def in_projection(x, mod, g1, w_in_bf, b_in):
    b, l, d = x.shape
    tm = 256
    outs = tuple(jax.ShapeDtypeStruct((b, l, w), F32) for _, w in IN_SPLITS)
    return pl.pallas_call(
        _inproj_kernel,
        out_shape=outs,
        grid=(b, l // tm),
        in_specs=[pl.BlockSpec((1, tm, d), lambda bi, i: (bi, i, 0)),
                  pl.BlockSpec((1, N_MOD, d), lambda bi, i: (bi, 0, 0)),
                  pl.BlockSpec((1, d), lambda bi, i: (0, 0)),
                  pl.BlockSpec((d, IN_COLS), lambda bi, i: (0, 0), pipeline_mode=pl.Buffered(1)),
                  pl.BlockSpec((1, IN_COLS), lambda bi, i: (0, 0))],
        out_specs=tuple(pl.BlockSpec((1, tm, w), lambda bi, i: (bi, i, 0)) for _, w in IN_SPLITS),
        compiler_params=_cp(("parallel", "parallel"), VMEM_LIMIT),
        name="in_projection",
    )(x, mod, g1.reshape(1, d), w_in_bf, b_in.reshape(1, IN_COLS))


def _hyena_pre_kernel(cur_ref, prev_ref, next_ref, w_ref, b_ref, x0_ref, z_ref):
    i = pl.program_id(1)
    last = pl.num_programs(1) - 1
    cur = cur_ref[0]
    tm = cur.shape[0]
    prev_row = jnp.where(i > 0, prev_ref[0, 7:8, :], 0.0)
    next_row = jnp.where(i < last, next_ref[0, 0:1, :], 0.0)
    row = lax.broadcasted_iota(I32, (tm, 1), 0)
    up = jnp.where(row == 0, prev_row, pltpu.roll(cur, 1, 0))
    dn = jnp.where(row == tm - 1, next_row, pltpu.roll(cur, tm - 1, 0))
    y = w_ref[0:1, :] * up + w_ref[1:2, :] * cur + w_ref[2:3, :] * dn + b_ref[...]
    c = HYENA_WIDTH
    x0_ref[0] = y[:, :c]
    z_ref[0] = y[:, c:2 * c] * y[:, 2 * c:]


def hyena_pre(hy, conv_w, conv_b):
    b, l, w = hy.shape
    tm = 512
    nb8 = l // 8
    sds = jax.ShapeDtypeStruct((b, l, HYENA_WIDTH), F32)
    return pl.pallas_call(
        _hyena_pre_kernel,
        out_shape=(sds, sds),
        grid=(b, l // tm),
        in_specs=[pl.BlockSpec((1, tm, w), lambda bi, i: (bi, i, 0)),
                  pl.BlockSpec((1, 8, w), lambda bi, i: (bi, jnp.maximum(i * (tm // 8) - 1, 0), 0)),
                  pl.BlockSpec((1, 8, w), lambda bi, i: (bi, jnp.minimum((i + 1) * (tm // 8), nb8 - 1), 0)),
                  pl.BlockSpec((3, w), lambda bi, i: (0, 0)),
                  pl.BlockSpec((1, w), lambda bi, i: (0, 0))],
        out_specs=(pl.BlockSpec((1, tm, HYENA_WIDTH), lambda bi, i: (bi, i, 0)),
                   pl.BlockSpec((1, tm, HYENA_WIDTH), lambda bi, i: (bi, i, 0))),
        compiler_params=_cp(("parallel", "parallel"), VMEM_LIMIT),
        name="hyena_pre",
    )(hy, hy, hy, conv_w, conv_b.reshape(1, w))


def _filter_kernel(w1_ref, b1_ref, w2_ref, b2_ref, w3_ref, b3_ref, w4_ref, fq_ref,
                   k_ref, asum_ref, *, seq):
    i = pl.program_id(0)
    tr = k_ref.shape[0]
    c = HYENA_WIDTH
    m = i * tr + lax.broadcasted_iota(I32, (tr, 1), 0)
    t_idx = jnp.where(m < seq, m, 2 * seq - m)
    tf = t_idx.astype(F32)
    t = tf * (1.0 / (seq - 1))
    omega = (2.0 * math.pi / seq) * tf
    lane = lax.broadcasted_iota(I32, (tr, LANES), 1)
    band = (lane % FILTER_BANDS).astype(F32)
    fr = 1e-4 + band * ((FILTER_BANDS - 1 - 1e-4) / (FILTER_BANDS - 1))
    arg = fr * omega
    feat = jnp.where(lane < FILTER_BANDS, jnp.cos(arg),
                     jnp.where(lane < 2 * FILTER_BANDS, -jnp.sin(arg),
                               jnp.where(lane == 2 * FILTER_BANDS, t, 0.0)))
    fq = fq_ref[...]

    def layer(h, w_ref_, b_ref_):
        w_hi, w_lo = _split(w_ref_[...])
        return jnp.sin(fq * (_dot3r(h, w_hi, w_lo) + b_ref_[...]))

    h = layer(feat, w1_ref, b1_ref)
    h = layer(h, w2_ref, b2_ref)
    h = layer(h, w3_ref, b3_ref)
    w4_hi, w4_lo = _split(w4_ref[...])
    h = _dot3r(h, w4_hi, w4_lo)
    ch = lax.broadcasted_iota(I32, (1, c), 1).astype(F32)
    max_decay = math.log(HYENA_TARGET) / HYENA_FAST_DECAY
    min_decay = math.log(HYENA_TARGET) / HYENA_SLOW_DECAY
    delta = jnp.abs(min_decay + ch * ((max_decay - min_decay) / (c - 1)))
    decay = jnp.exp(-t * delta)
    k = jnp.where(m < seq, h[:, :c], jnp.where(m > seq, h[:, c:], 0.0)) * decay
    k_ref[...] = k

    @pl.when(i == 0)
    def _():
        asum_ref[...] = jnp.zeros_like(asum_ref)

    asum_ref[...] += jnp.sum(jnp.abs(k), axis=0, keepdims=True)


def hyena_filter_taps(seq, w1, b1, w2, b2, w3, b3, w4, fq):
    c = HYENA_WIDTH
    tr = 512
    fo = FILTER_ORDER
    w1p = jnp.concatenate([w1[1:1 + 2 * FILTER_BANDS], w1[0:1],
                           jnp.zeros((LANES - 2 * FILTER_BANDS - 1, fo), F32)], axis=0)
    full = lambda shape: pl.BlockSpec(shape, lambda i: (0, 0))
    return pl.pallas_call(
        functools.partial(_filter_kernel, seq=seq),
        out_shape=(jax.ShapeDtypeStruct((2 * seq, c), F32), jax.ShapeDtypeStruct((1, c), F32)),
        grid=(2 * seq // tr,),
        in_specs=[full((LANES, fo)), full((1, fo)), full((fo, fo)), full((1, fo)),
                  full((fo, fo)), full((1, fo)), full((fo, 2 * c)), full((1, fo))],
        out_specs=(pl.BlockSpec((tr, c), lambda i: (i, 0)), full((1, c))),
        compiler_params=_cp(("arbitrary",), VMEM_LIMIT),
        name="hyena_filter",
    )(w1p, b1.reshape(1, fo), w2, b2.reshape(1, fo), w3, b3.reshape(1, fo),
      w4, fq.reshape(1, fo))


def _dft_mats(n):
    k = np.arange(n)
    ang = -2.0 * np.pi * np.outer(k, k) / n
    return np.cos(ang), np.sin(ang)


def _split_np(m):
    m32 = jnp.asarray(m, F32)
    return _split(m32)


def _stage_outer_mats(n1):
    fr, fi = _dft_mats(n1)
    h = n1 // 2
    fwd_pair = np.block([[fr[:, :h], -fi[:, :h]], [fi[:, :h], fr[:, :h]]])
    fwd_real = np.concatenate([fr, fi], axis=0)
    inv_pair = np.block([[fr[:h, :], fi[:h, :]], [-fi[:h, :], fr[:h, :]]])
    return _split_np(fwd_pair), _split_np(fwd_real), _split_np(inv_pair)


def _stage_inner_mats(n2):
    fr, fi = _dft_mats(n2)
    fwd = np.block([[fr, -fi], [fi, fr]])
    inv = np.block([[fr, fi], [-fi, fr]])
    return _split_np(fwd), _split_np(inv)


def _left_dft_kernel(mh_ref, ml_ref, x_ref, o_ref):
    o_ref[0] = _dot3(mh_ref[...], ml_ref[...], x_ref[0])


def left_dft(mats, x, tc=1536):
    m_hi, m_lo = mats
    r, k = m_hi.shape
    p, _, cols = x.shape
    return pl.pallas_call(
        _left_dft_kernel,
        out_shape=jax.ShapeDtypeStruct((p, r, cols), F32),
        grid=(p, cols // tc),
        in_specs=[pl.BlockSpec((r, k), lambda pi, j: (0, 0)),
                  pl.BlockSpec((r, k), lambda pi, j: (0, 0)),
                  pl.BlockSpec((1, k, tc), lambda pi, j: (pi, 0, j))],
        out_specs=pl.BlockSpec((1, r, tc), lambda pi, j: (pi, 0, j)),
        compiler_params=_cp(("parallel", "parallel"), VMEM_LIMIT),
        name="hyena_dft_outer",
    )(m_hi, m_lo, x)


def _left_idft_kernel(mh_ref, ml_ref, y_ref, x0_ref, z_ref, d_ref, o_ref):
    conv = _dot3(mh_ref[...], ml_ref[...], y_ref[0])
    z = z_ref[0]
    o_ref[0] = x0_ref[0] * (conv + z * d_ref[...])


def left_idft_gate(mats, spec, x0, z, d_tiled, tc=1536):
    m_hi, m_lo = mats
    r, k = m_hi.shape
    p, _, cols = spec.shape
    return pl.pallas_call(
        _left_idft_kernel,
        out_shape=jax.ShapeDtypeStruct((p, r, cols), F32),
        grid=(p, cols // tc),
        in_specs=[pl.BlockSpec((r, k), lambda pi, j: (0, 0)),
                  pl.BlockSpec((r, k), lambda pi, j: (0, 0)),
                  pl.BlockSpec((1, k, tc), lambda pi, j: (pi, 0, j)),
                  pl.BlockSpec((1, r, tc), lambda pi, j: (pi, 0, j)),
                  pl.BlockSpec((1, r, tc), lambda pi, j: (pi, 0, j)),
                  pl.BlockSpec((1, tc), lambda pi, j: (0, j))],
        out_specs=pl.BlockSpec((1, r, tc), lambda pi, j: (pi, 0, j)),
        compiler_params=_cp(("parallel", "parallel"), VMEM_LIMIT),
        name="hyena_idft_outer",
    )(m_hi, m_lo, spec, x0, z, d_tiled)


def _twiddle(k1, n2, n):
    row = lax.broadcasted_iota(I32, (n2, LANES), 0)
    ang = ((row * k1) & (n - 1)).astype(F32) * (2.0 * math.pi / n)
    reps = HYENA_WIDTH // LANES
    c = jnp.concatenate([jnp.cos(ang)] * reps, axis=1)
    s = jnp.concatenate([jnp.sin(ang)] * reps, axis=1)
    return c, s


def _spec_filter_kernel(fh_ref, fl_ref, a_ref, scale_ref, o_ref, *, n):
    n2 = a_ref.shape[3]
    c, s = _twiddle(pl.program_id(0), n2, n)
    ar, ai = a_ref[0, 0, 0], a_ref[0, 1, 0]
    x = jnp.concatenate([ar * c + ai * s, ai * c - ar * s], axis=0)
    y = _dot3(fh_ref[...], fl_ref[...], x) * scale_ref[...]
    o_ref[0, 0] = y[:n2]
    o_ref[1, 0] = y[n2:]


def filter_spectrum(inner_fwd, a, scale, n):
    _, _, n1, n2, c = a.shape
    fh, fl = inner_fwd
    return pl.pallas_call(
        functools.partial(_spec_filter_kernel, n=n),
        out_shape=jax.ShapeDtypeStruct((2, n1, n2, c), F32),
        grid=(n1,),
        in_specs=[pl.BlockSpec((2 * n2, 2 * n2), lambda k: (0, 0)),
                  pl.BlockSpec((2 * n2, 2 * n2), lambda k: (0, 0)),
                  pl.BlockSpec((1, 2, 1, n2, c), lambda k: (0, 0, k, 0, 0)),
                  pl.BlockSpec((1, c), lambda k: (0, 0))],
        out_specs=pl.BlockSpec((2, 1, n2, c), lambda k: (0, k, 0, 0)),
        compiler_params=_cp(("parallel",), VMEM_LIMIT),
        name="hyena_filter_spectrum",
    )(fh, fl, a, scale)


def _spec_conv_kernel(fh_ref, fl_ref, ih_ref, il_ref, a_ref, k_ref, o_ref, *, n):
    n2 = a_ref.shape[3]
    c, s = _twiddle(pl.program_id(1), n2, n)
    ar, ai = a_ref[0, 0, 0], a_ref[0, 1, 0]
    x = jnp.concatenate([ar * c + ai * s, ai * c - ar * s], axis=0)
    y = _dot3(fh_ref[...], fl_ref[...], x)
    yr, yi = y[:n2], y[n2:]
    kr, ki = k_ref[0, 0], k_ref[1, 0]
    p = jnp.concatenate([yr * kr - yi * ki, yr * ki + yi * kr], axis=0)
    w = _dot3(ih_ref[...], il_ref[...], p)
    wr, wi = w[:n2], w[n2:]
    o_ref[0, 0, 0] = wr * c - wi * s
    o_ref[0, 1, 0] = wr * s + wi * c


def spectrum_convolve(inner, a, kspec, n):
    p, _, n1, n2, c = a.shape
    (fh, fl), (ih, il) = inner
    msp = pl.BlockSpec((2 * n2, 2 * n2), lambda pi, k: (0, 0))
    return pl.pallas_call(
        functools.partial(_spec_conv_kernel, n=n),
        out_shape=jax.ShapeDtypeStruct((p, 2, n1, n2, c), F32),
        grid=(p, n1),
        in_specs=[msp, msp, msp, msp,
                  pl.BlockSpec((1, 2, 1, n2, c), lambda pi, k: (pi, 0, k, 0, 0)),
                  pl.BlockSpec((2, 1, n2, c), lambda pi, k: (0, k, 0, 0))],
        out_specs=pl.BlockSpec((1, 2, 1, n2, c), lambda pi, k: (pi, 0, k, 0, 0)),
        compiler_params=_cp(("parallel", "parallel"), VMEM_LIMIT),
        name="hyena_spectrum_convolve",
    )(fh, fl, ih, il, a, kspec)


def hyena_filter_spectrum(seq, p):
    n = 2 * seq
    n1, n2 = _fft_factors(n)
    c = HYENA_WIDTH
    taps, asum = hyena_filter_taps(seq, p['filt_w1'], p['filt_b1'], p['filt_w2'], p['filt_b2'],
                                   p['filt_w3'], p['filt_b3'], p['filt_w4'], p['filt_freq'])
    _, fwd_real, _ = _stage_outer_mats(n1)
    a = left_dft(fwd_real, taps.reshape(1, n1, n2 * c))
    inner_fwd, _ = _stage_inner_mats(n2)
    scale = 1.0 / (asum * n)
    return filter_spectrum(inner_fwd, a.reshape(1, 2, n1, n2, c), scale, n)


def hyena_long_conv(x0, z, kspec, hy_d):
    b, l, c = z.shape
    n = 2 * l
    n1, n2 = _fft_factors(n)
    fwd_pair, _, inv_pair = _stage_outer_mats(n1)
    inner = _stage_inner_mats(n2)
    zp = z.reshape(b // 2, n1, n2 * c)
    a = left_dft(fwd_pair, zp)
    w = spectrum_convolve(inner, a.reshape(b // 2, 2, n1, n2, c), kspec, n)
    d_tiled = jnp.tile(hy_d.reshape(1, c), (1, n2))
    y = left_idft_gate(inv_pair, w.reshape(b // 2, 2 * n1, n2 * c),
                       x0.reshape(b // 2, n1, n2 * c), zp, d_tiled)
    return y.reshape(b, l, c)


def _rope_table_kernel(inv_ref, cos_ref, sin_ref):
    tm = cos_ref.shape[0]
    pos = (pl.program_id(0) * tm + lax.broadcasted_iota(I32, (tm, LANES), 0)).astype(F32)
    ang = pos * inv_ref[...]
    lane = lax.broadcasted_iota(I32, (tm, LANES), 1)
    cos_ref[...] = jnp.cos(ang)
    sin_ref[...] = jnp.where(lane % HEAD_DIM < HEAD_DIM // 2, -jnp.sin(ang), jnp.sin(ang))


def rope_tables(seq):
    inv = ROPE_THETA ** (-jnp.arange(0, HEAD_DIM, 2, dtype=F32) / HEAD_DIM)
    inv128 = jnp.tile(inv, LANES // (HEAD_DIM // 2)).reshape(1, LANES)
    tm = 512
    sds = jax.ShapeDtypeStruct((seq, LANES), F32)
    return pl.pallas_call(
        _rope_table_kernel,
        out_shape=(sds, sds),
        grid=(seq // tm,),
        in_specs=[pl.BlockSpec((1, LANES), lambda i: (0, 0))],
        out_specs=(pl.BlockSpec((tm, LANES), lambda i: (i, 0)), pl.BlockSpec((tm, LANES), lambda i: (i, 0))),
        compiler_params=_cp(("parallel",), VMEM_LIMIT),
        name="rope_tables",
    )(inv128)


def _norm_rope(t, g, cos, sin_signed, ones_bd, scale):
    sq = t * t
    parts = []
    for c0 in range(0, ATTN_WIDTH, 256):
        hi, lo = _split(sq[:, c0:c0 + 256])
        parts.append(_dot(hi, ones_bd) + _dot(lo, ones_bd))
    ms = jnp.concatenate(parts, axis=1) * (1.0 / HEAD_DIM)
    tn = t * lax.rsqrt(ms + EPS) * g
    lane = lax.broadcasted_iota(I32, (t.shape[0], LANES), 1)
    first_half = lane % HEAD_DIM < HEAD_DIM // 2
    outs = []
    for c0 in range(0, ATTN_WIDTH, LANES):
        x = tn[:, c0:c0 + LANES]
        rot = jnp.where(first_half, pltpu.roll(x, LANES - HEAD_DIM // 2, 1), pltpu.roll(x, HEAD_DIM // 2, 1))
        outs.append((x * cos + rot * sin_signed) * scale)
    return jnp.concatenate(outs, axis=1)


def _qk_prep_kernel(qkv_ref, cos_ref, sin_ref, qg_ref, kg_ref, bd_ref, q_ref, k_ref, v_ref):
    w = ATTN_WIDTH
    cos, sin = cos_ref[...], sin_ref[...]
    bd = bd_ref[...]
    q_ref[0] = _norm_rope(qkv_ref[0, :, :w], qg_ref[...], cos, sin, bd, HEAD_DIM ** -0.5).astype(BF16)
    k_ref[0] = _norm_rope(qkv_ref[0, :, w:2 * w], kg_ref[...], cos, sin, bd, 1.0).astype(BF16)
    v_ref[0] = qkv_ref[0, :, 2 * w:].astype(BF16)


def qk_prep(qkv, cos, sin, q_g, k_g):
    b, l, _ = qkv.shape
    tm = 512
    w = ATTN_WIDTH
    head = np.arange(256) // HEAD_DIM
    ones_bd = jnp.asarray(head[:, None] == head[None, :], BF16)
    sds = jax.ShapeDtypeStruct((b, l, w), BF16)
    tile_g = lambda g: jnp.tile(g, w // HEAD_DIM).reshape(1, w)
    ospec = pl.BlockSpec((1, tm, w), lambda bi, i: (bi, i, 0))
    return pl.pallas_call(
        _qk_prep_kernel,
        out_shape=(sds, sds, sds),
        grid=(b, l // tm),
        in_specs=[pl.BlockSpec((1, tm, 3 * w), lambda bi, i: (bi, i, 0)),
                  pl.BlockSpec((tm, LANES), lambda bi, i: (i, 0)),
                  pl.BlockSpec((tm, LANES), lambda bi, i: (i, 0)),
                  pl.BlockSpec((1, w), lambda bi, i: (0, 0)),
                  pl.BlockSpec((1, w), lambda bi, i: (0, 0)),
                  pl.BlockSpec((256, 256), lambda bi, i: (0, 0))],
        out_specs=(ospec, ospec, ospec),
        compiler_params=_cp(("parallel", "parallel"), VMEM_LIMIT),
        name="qk_norm_rope",
    )(qkv, cos, sin, tile_g(q_g), tile_g(k_g), ones_bd)


def _band_attn_kernel(q_ref, kp_ref, kc_ref, kn_ref, vp_ref, vc_ref, vn_ref, o_ref, lse_ref, *, m_len):
    i = pl.program_id(2)
    ts = q_ref.shape[1]
    h = ATTN_HALF
    tq = min(ATTN_QUERY_TILE, ts)
    kk = jnp.concatenate([kp_ref[0], kc_ref[0], kn_ref[0]], axis=0)
    vv = jnp.concatenate([vp_ref[0], vc_ref[0], vn_ref[0]], axis=0)
    qi = lax.broadcasted_iota(I32, (tq, tq + 2 * h), 0)
    kj = lax.broadcasted_iota(I32, (tq, tq + 2 * h), 1)
    in_band = jnp.abs(kj - h - qi) <= h
    for q0 in range(0, ts, tq):
        q = q_ref[0, q0:q0 + tq, :]
        kpos = i * ts + q0 - h + kj
        mask = in_band & (kpos >= 0) & (kpos < m_len)
        outs, lses = [], []
        for hd in range(HEADS_PER_GROUP):
            sl = slice(hd * HEAD_DIM, (hd + 1) * HEAD_DIM)
            s = lax.dot_general(q[:, sl], kk[q0:q0 + tq + 2 * h, sl], (((1,), (1,)), ((), ())),
                                preferred_element_type=F32)
            s = jnp.where(mask, s, NEG_BIG)
            mx = jnp.max(s, axis=1, keepdims=True)
            p = jnp.exp(s - mx)
            den = jnp.sum(p, axis=1, keepdims=True)
            o = _dot(p.astype(BF16), vv[q0:q0 + tq + 2 * h, sl]) / den
            outs.append(o)
            lses.append(jnp.broadcast_to(mx + jnp.log(den), (tq, HEAD_DIM)))
        o_ref[0, q0:q0 + tq, :] = jnp.concatenate(outs, axis=1)
        lse_ref[0, q0:q0 + tq, :] = jnp.concatenate(lses, axis=1)


def band_attention(q, k, v, group, dil):
    b, l, w = q.shape
    m_len = l // dil
    tq = min(ATTN_STEP_ROWS, m_len)
    hb = ATTN_HALF
    gw = ATTN_OUT
    nhb = m_len // hb
    view = lambda t: t.reshape(b, m_len, dil * w)
    col = lambda r: r * (w // gw) + group
    cur = pl.BlockSpec((1, tq, gw), lambda bi, r, i: (bi, i, col(r)))
    prev = pl.BlockSpec((1, hb, gw), lambda bi, r, i: (bi, jnp.maximum(i * (tq // hb) - 1, 0), col(r)))
    nxt = pl.BlockSpec((1, hb, gw), lambda bi, r, i: (bi, jnp.minimum((i + 1) * (tq // hb), nhb - 1), col(r)))
    ospec = pl.BlockSpec((1, tq, gw), lambda bi, r, i: (bi, i, r))
    sds = jax.ShapeDtypeStruct((b, m_len, dil * gw), F32)
    qv, kv, vv = view(q), view(k), view(v)
    o, lse = pl.pallas_call(
        functools.partial(_band_attn_kernel, m_len=m_len),
        out_shape=(sds, sds),
        grid=(b, dil, m_len // tq),
        in_specs=[cur, prev, cur, nxt, prev, cur, nxt],
        out_specs=(ospec, ospec),
        compiler_params=_cp(("parallel", "parallel", "parallel"), VMEM_LIMIT),
        name=f"band_attention_d{dil}",
    )(qv, kv, kv, kv, vv, vv, vv)
    return o.reshape(b, l, gw), lse.reshape(b, l, gw)


def _pooled(a_cur, a_prev, a_next, i, last, seq):
    tm = a_cur.shape[0]
    hl = POOL_HALO
    prev = jnp.where(i > 0, a_prev, 0.0)
    nxt = jnp.where(i < last, a_next, 0.0)
    ext = jnp.concatenate([prev, a_cur, nxt], axis=0)
    rows = ext.shape[0]
    lane = lax.broadcasted_iota(I32, (1, POOL_WIDTH), 1)
    pos = i * tm + lax.broadcasted_iota(I32, (tm, 1), 0)
    trail = ext
    win = None
    count = None
    for gi, w in enumerate(POOL_WINDOWS):
        trail = trail + pltpu.roll(trail, w // 2, 0)
        left, right = w // 2, w - 1 - w // 2
        centred = pltpu.roll(trail, rows - right, 0) if right else trail
        centred = centred[hl:hl + tm]
        cnt = (jnp.minimum(pos + right, seq - 1) - jnp.maximum(pos - left, 0) + 1).astype(F32)
        if win is None:
            win, count = centred, jnp.broadcast_to(cnt, (tm, POOL_WIDTH))
        else:
            sel = lane >= gi * POOL_GROUP
            win = jnp.where(sel, centred, win)
            count = jnp.where(sel, cnt, count)
    return win / count - a_cur


def _merge_kernel(a_ref, ap_ref, an_ref, yb_ref, o0_ref, o1_ref, o2_ref, l0_ref, l1_ref, l2_ref,
                  gt_ref, x_ref, mod_ref, pw_ref, ps_ref, wa_ref, wb_ref, wc_ref, wo_ref, out_ref, *, seq):
    i = pl.program_id(1)
    last = pl.num_programs(1) - 1
    d = D_MODEL
    pooled = _pooled(a_ref[0], ap_ref[0], an_ref[0], i, last, seq)
    ya = _dot(pooled.astype(BF16), pw_ref[...]) * ps_ref[...]
    l0, l1, l2 = l0_ref[0], l1_ref[0], l2_ref[0]
    mx = jnp.maximum(jnp.maximum(l0, l1), l2)
    e0, e1, e2 = jnp.exp(l0 - mx), jnp.exp(l1 - mx), jnp.exp(l2 - mx)
    yc = (e0 * o0_ref[0] + e1 * o1_ref[0] + e2 * o2_ref[0]) / (e0 + e1 + e2)
    merged = (jax.nn.sigmoid(gt_ref[0, :, :d]) * _dot(ya.astype(BF16), wa_ref[...])
              + jax.nn.sigmoid(gt_ref[0, :, d:2 * d]) * _dot(yb_ref[0].astype(BF16), wb_ref[...])
              + jax.nn.sigmoid(gt_ref[0, :, 2 * d:]) * _dot(yc.astype(BF16), wc_ref[...]))
    out_ref[0] = x_ref[0] + mod_ref[0, 2:3, :] * _dot(merged.astype(BF16), wo_ref[...])


def merge_branches(a, yb, attn, gates, x, mod, pool_bd, pool_scale, wa, wb, wc, wo):
    b, l, d = x.shape
    tm = 256
    hl = POOL_HALO
    nh = l // hl
    (o0, l0), (o1, l1), (o2, l2) = attn
    row = lambda w: pl.BlockSpec((1, tm, w), lambda bi, i: (bi, i, 0))
    full = lambda s: pl.BlockSpec(s, lambda bi, i: (0, 0))
    return pl.pallas_call(
        functools.partial(_merge_kernel, seq=l),
        out_shape=jax.ShapeDtypeStruct((b, l, d), F32),
        grid=(b, l // tm),
        in_specs=[row(POOL_WIDTH),
                  pl.BlockSpec((1, hl, POOL_WIDTH), lambda bi, i: (bi, jnp.maximum(i * (tm // hl) - 1, 0), 0)),
                  pl.BlockSpec((1, hl, POOL_WIDTH), lambda bi, i: (bi, jnp.minimum((i + 1) * (tm // hl), nh - 1), 0)),
                  row(HYENA_WIDTH), row(ATTN_OUT), row(ATTN_OUT), row(ATTN_OUT),
                  row(ATTN_OUT), row(ATTN_OUT), row(ATTN_OUT), row(3 * d), row(d),
                  pl.BlockSpec((1, N_MOD, d), lambda bi, i: (bi, 0, 0)),
                  full((POOL_WIDTH, POOL_WIDTH)), full((1, POOL_WIDTH)), full((POOL_WIDTH, d)),
                  full((HYENA_WIDTH, d)), full((ATTN_OUT, d)), full((d, d))],
        out_specs=row(d),
        compiler_params=_cp(("parallel", "parallel"), VMEM_LIMIT),
        name="merge_branches",
    )(a, a, a, yb, o0, o1, o2, l0, l1, l2, gates, x, mod, pool_bd, pool_scale, wa, wb, wc, wo)


def _router_kernel(x_ref, mod_ref, g_ref, wr_ref, br_ref, tri_ref, h_ref, idx_ref, gw_ref, rank_ref,
                   cnt_ref, carry_ref):
    first = (pl.program_id(0) == 0) & (pl.program_id(1) == 0)

    @pl.when(first)
    def _():
        carry_ref[...] = jnp.zeros_like(carry_ref)

    h = _modulated_norm(x_ref[0], g_ref[...], mod_ref[0, 4:5, :], mod_ref[0, 3:4, :])
    h_ref[0] = h
    w_hi, w_lo = _split(wr_ref[...])
    logits = _dot3r(h, w_hi, w_lo) + br_ref[...]
    tm, ne = logits.shape
    lane = lax.broadcasted_iota(I32, (tm, ne), 1).astype(F32)
    lane_out = lax.broadcasted_iota(I32, (tm, LANES), 1)
    work = logits
    vals, idxs = [], []
    member = jnp.zeros((tm, ne), F32)
    for _ in range(TOP_K):
        mx = jnp.max(work, axis=1, keepdims=True)
        idx = jnp.min(jnp.where(work == mx, lane, float(ne)), axis=1, keepdims=True)
        hit = lane == idx
        member = jnp.where(hit, 1.0, member)
        work = jnp.where(hit, -jnp.inf, work)
        vals.append(mx)
        idxs.append(idx)
    exps = [jnp.exp(v - vals[0]) for v in vals]
    den = exps[0] + exps[1] + exps[2] + exps[3]
    prefix = _dot(tri_ref[...], member.astype(BF16)) + carry_ref[...]
    idx_out = jnp.zeros((tm, LANES), I32)
    gw_out = jnp.zeros((tm, LANES), F32)
    rank_out = jnp.zeros((tm, LANES), I32)
    for k in range(TOP_K):
        rk = jnp.sum(jnp.where(lane == idxs[k], prefix, 0.0), axis=1, keepdims=True)
        idx_out = jnp.where(lane_out == k, idxs[k].astype(I32), idx_out)
        gw_out = jnp.where(lane_out == k, exps[k] / den, gw_out)
        rank_out = jnp.where(lane_out == k, rk.astype(I32), rank_out)
    idx_ref[0] = idx_out
    gw_ref[0] = gw_out
    rank_ref[0] = rank_out
    carry_ref[...] += jnp.sum(member, axis=0, keepdims=True)
    cnt_ref[...] = carry_ref[...]


def moe_router(x, mod, g2, w_router, b_router):
    b, l, d = x.shape
    tm = 256
    ne = N_EXPERTS
    tri = jnp.asarray(np.tril(np.ones((tm, tm), np.float32), -1), BF16)
    row = lambda w: pl.BlockSpec((1, tm, w), lambda bi, i: (bi, i, 0))
    full = lambda s: pl.BlockSpec(s, lambda bi, i: (0, 0))
    return pl.pallas_call(
        _router_kernel,
        out_shape=(jax.ShapeDtypeStruct((b, l, d), F32),
                   jax.ShapeDtypeStruct((b, l, LANES), I32),
                   jax.ShapeDtypeStruct((b, l, LANES), F32),
                   jax.ShapeDtypeStruct((b, l, LANES), I32),
                   jax.ShapeDtypeStruct((1, ne), F32)),
        grid=(b, l // tm),
        in_specs=[row(d), pl.BlockSpec((1, N_MOD, d), lambda bi, i: (bi, 0, 0)), full((1, d)),
                  full((d, ne)), full((1, ne)), full((tm, tm))],
        out_specs=(row(d), row(LANES), row(LANES), row(LANES), full((1, ne))),
        scratch_shapes=[pltpu.VMEM((1, ne), F32)],
        compiler_params=_cp(("arbitrary", "arbitrary"), VMEM_LIMIT),
        name="moe_router",
    )(x, mod, g2.reshape(1, d), w_router, b_router.reshape(1, ne), tri)


def _row_copy(src, src_row, dst, dst_row, sem):
    return pltpu.make_async_copy(src.at[pl.ds(src_row, 1), :], dst.at[pl.ds(dst_row, 1), :], sem)


def _start_gather(h_hbm, tok_ref, xbuf, sem):
    for r in range(xbuf.shape[0]):
        _row_copy(h_hbm, tok_ref[0, 0, r], xbuf, r, sem).start()


def _start_scatter(obuf, slot_ref, out_hbm, sem):
    for r in range(obuf.shape[0]):
        _row_copy(obuf, r, out_hbm, slot_ref[0, 0, r], sem).start()


def _wait_block(hbm, buf, sem):
    pltpu.make_async_copy(hbm.at[pl.ds(0, buf.shape[0]), :], buf, sem).wait()


def _expert_ffn(x, w1_ref, b1_ref, w2_ref, b2_ref):
    f = D_MODEL
    u = _dot(x.astype(BF16), w1_ref[0]) + b1_ref[0]
    x_glu = jnp.minimum(u[:, :f], SWIGLU_LIMIT)
    x_lin = jnp.clip(u[:, f:], -SWIGLU_LIMIT, SWIGLU_LIMIT)
    act = x_glu * jax.nn.sigmoid(SWIGLU_ALPHA * x_glu) * (x_lin + 1.0)
    return _dot(act.astype(BF16), w2_ref[0]) + b2_ref[0]


def _expert_kernel(blk_e_ref, tok_ref, tokn_ref, slot_ref, h_hbm, w1_ref, b1_ref, w2_ref, b2_ref, out_hbm,
                   xa, xb, oa, ob, gsem, ssem):
    i = pl.program_id(0)
    last = pl.num_programs(0) - 1

    @pl.when(i == 0)
    def _():
        _start_gather(h_hbm, tok_ref, xa, gsem.at[0])

    def step(par, x_cur, x_next, o_cur, o_prev):
        @pl.when((i % 2 == par) & (i >= 2))
        def _():
            _wait_block(out_hbm, o_cur, ssem.at[par])

        @pl.when(i % 2 == par)
        def _():
            _wait_block(h_hbm, x_cur, gsem.at[par])
            _start_gather(h_hbm, tokn_ref, x_next, gsem.at[1 - par])
            o_cur[...] = _expert_ffn(x_cur[...], w1_ref, b1_ref, w2_ref, b2_ref)
            _start_scatter(o_cur, slot_ref, out_hbm, ssem.at[par])

        @pl.when((i == last) & (i % 2 == par))
        def _():
            _wait_block(h_hbm, x_next, gsem.at[1 - par])
            _wait_block(out_hbm, o_cur, ssem.at[par])
            _wait_block(out_hbm, o_prev, ssem.at[1 - par])

    step(0, xa, xb, oa, ob)
    step(1, xb, xa, ob, oa)


def moe_experts(h_flat, row_tok, row_slot, blk_e, n_slots, w1_bf, b1, w2_bf, b2):
    t, d = h_flat.shape
    nblk, _, bm = row_tok.shape
    assert nblk >= 2
    ne, _, f2 = w1_bf.shape
    meta = blk_e.astype(I32)
    smem = lambda imap: pl.BlockSpec((1, 1, bm), imap, memory_space=pltpu.SMEM)
    grid_spec = pltpu.PrefetchScalarGridSpec(
        num_scalar_prefetch=1,
        grid=(nblk,),
        in_specs=[smem(lambda i, m: (i, 0, 0)),
                  smem(lambda i, m: (jnp.minimum(i + 1, nblk - 1), 0, 0)),
                  smem(lambda i, m: (i, 0, 0)),
                  pl.BlockSpec(memory_space=pl.ANY),
                  pl.BlockSpec((1, d, f2), lambda i, m: (m[i], 0, 0)),
                  pl.BlockSpec((1, 1, f2), lambda i, m: (m[i], 0, 0)),
                  pl.BlockSpec((1, f2 // 2, d), lambda i, m: (m[i], 0, 0)),
                  pl.BlockSpec((1, 1, d), lambda i, m: (m[i], 0, 0))],
        out_specs=pl.BlockSpec(memory_space=pl.ANY),
        scratch_shapes=[pltpu.VMEM((bm, d), F32), pltpu.VMEM((bm, d), F32),
                        pltpu.VMEM((bm, d), F32), pltpu.VMEM((bm, d), F32),
                        pltpu.SemaphoreType.DMA((2,)), pltpu.SemaphoreType.DMA((2,))],
    )
    return pl.pallas_call(
        _expert_kernel,
        out_shape=jax.ShapeDtypeStruct((n_slots, d), F32),
        grid_spec=grid_spec,
        compiler_params=pltpu.CompilerParams(dimension_semantics=("arbitrary",), vmem_limit_bytes=VMEM_LIMIT,
                                             disable_bounds_checks=True),
        name="moe_experts",
    )(meta, row_tok, row_tok, row_slot, h_flat, w1_bf, b1.reshape(ne, 1, f2), w2_bf, b2.reshape(ne, 1, d))


def _combine_kernel(x_ref, s0_ref, s1_ref, s2_ref, s3_ref, gw_ref, mod_ref, o_ref):
    gw = gw_ref[0]
    acc = gw[:, 0:1] * s0_ref[...]
    for k, s_ref in enumerate((s1_ref, s2_ref, s3_ref), start=1):
        acc = acc + gw[:, k:k + 1] * s_ref[...]
    o_ref[0] = x_ref[0] + mod_ref[0, 5:6, :] * acc


def moe_combine(x, slots, gate_w, mod):
    b, l, d = x.shape
    tm = 256
    nl = l // tm
    nt = b * nl
    plane = lambda k: pl.BlockSpec((tm, d), lambda bi, i: (k * nt + bi * nl + i, 0))
    row = lambda w: pl.BlockSpec((1, tm, w), lambda bi, i: (bi, i, 0))
    return pl.pallas_call(
        _combine_kernel,
        out_shape=jax.ShapeDtypeStruct((b, l, d), F32),
        grid=(b, nl),
        in_specs=[row(d), plane(0), plane(1), plane(2), plane(3), row(LANES),
                  pl.BlockSpec((1, N_MOD, d), lambda bi, i: (bi, 0, 0))],
        out_specs=row(d),
        compiler_params=_cp(("parallel", "parallel"), VMEM_LIMIT),
        name="moe_combine",
    )(x, slots, slots, slots, slots, gate_w, mod)


def moe_block(x, mod, p):
    b, l, d = x.shape
    t = b * l
    bm = MOE_ROWS
    n_assign = t * TOP_K
    h, top_i, gate_w, rank, counts = moe_router(x, mod, p['norm2_g'], p['w_router'], p['b_router'])
    top_i = top_i.reshape(t, LANES)[:, :TOP_K]
    rank = rank.reshape(t, LANES)[:, :TOP_K]
    counts = counts.reshape(N_EXPERTS).astype(I32)
    pcounts = (counts + bm - 1) // bm * bm
    pends = jnp.cumsum(pcounts)
    pstarts = pends - pcounts
    dest = pstarts[top_i] + rank
    nblk = n_assign // bm + N_EXPERTS
    n_rows = nblk * bm
    slot_tk = (jnp.arange(TOP_K, dtype=I32)[None, :] * t + jnp.arange(t, dtype=I32)[:, None])
    row_slot = jnp.full((n_rows,), -1, I32).at[dest.reshape(-1)].set(slot_tk.reshape(-1))
    is_pad = row_slot < 0
    row_tok = jnp.where(is_pad, 0, row_slot % t)
    row_slot = jnp.where(is_pad, n_assign - 1 + jnp.cumsum(is_pad.astype(I32)), row_slot)
    blk_e = jnp.minimum(jnp.searchsorted(pends, jnp.arange(nblk, dtype=I32) * bm, side='right'),
                        N_EXPERTS - 1)
    slots = moe_experts(h.reshape(t, d), row_tok.reshape(nblk, 1, bm), row_slot.reshape(nblk, 1, bm), blk_e,
                        n_rows, p['w_mlp1'], p['b_mlp1'], p['w_mlp2'], p['b_mlp2'])
    return moe_combine(x, slots, gate_w, mod)


def token_mixer_block(x, mod, p, cos, sin, kspec):
    a, hy, qkv, gates = in_projection(x, mod, p['norm1_g'], p['w_in'], p['b_in'])
    x0, z = hyena_pre(hy, p['hy_conv_w'], p['hy_conv_b'])
    yb = hyena_long_conv(x0, z, kspec, p['hy_d'])
    q, k, v = qk_prep(qkv, cos, sin, p['q_norm_g'], p['k_norm_g'])
    attn = [band_attention(q, k, v, gi, dil) for gi, (_, dil) in enumerate(ATTN_GROUPS)]
    return merge_branches(a, yb, attn, gates, x, mod, p['pool_bd'], p['pool_scale'],
                          p['w_branch_a'], p['w_branch_b'], p['w_branch_c'], p['w_o'])


def _pool_block_diag(pool_w):
    g = POOL_GROUP
    bd = jnp.zeros((POOL_WIDTH, POOL_WIDTH), F32)
    for gi in range(len(POOL_WINDOWS)):
        bd = bd.at[gi * g:(gi + 1) * g, gi * g:(gi + 1) * g].set(pool_w[gi])
    return bd.astype(BF16)


def kernel(x_prompt, x_sample, c_prompt, c_sample, ada_w, ada_b, norm1_g, w_in, b_in, pool_w, pool_scale, hy_conv_w, hy_conv_b, filt_w1, filt_b1, filt_w2, filt_b2, filt_w3, filt_b3, filt_w4, filt_freq, hy_d, q_norm_g, k_norm_g, w_branch_a, w_branch_b, w_branch_c, w_o, norm2_g, w_router, b_router, w_mlp1, b_mlp1, w_mlp2, b_mlp2):
    depth = ada_w.shape[0]
    d = D_MODEL
    groups = [(x_prompt, c_prompt), (x_sample, c_sample)]
    nb = [x.shape[0] for x, _ in groups]
    c_all = jnp.concatenate([c for _, c in groups], axis=0)
    pad = (-c_all.shape[0]) % 8
    c_all = jnp.pad(c_all, ((0, pad), (0, 0)))
    mod_all = ada_modulation(c_all, ada_w, ada_b)
    tables = {}
    for x, _ in groups:
        l = x.shape[1]
        if l not in tables:
            tables[l] = rope_tables(l)
    ys = [x for x, _ in groups]
    for layer in range(depth):
        p = {
            'norm1_g': norm1_g[layer], 'w_in': w_in[layer].astype(BF16), 'b_in': b_in[layer],
            'pool_bd': _pool_block_diag(pool_w[layer]), 'pool_scale': pool_scale[layer].reshape(1, POOL_WIDTH),
            'hy_conv_w': hy_conv_w[layer], 'hy_conv_b': hy_conv_b[layer],
            'filt_w1': filt_w1[layer], 'filt_b1': filt_b1[layer], 'filt_w2': filt_w2[layer],
            'filt_b2': filt_b2[layer], 'filt_w3': filt_w3[layer], 'filt_b3': filt_b3[layer],
            'filt_w4': filt_w4[layer], 'filt_freq': filt_freq[layer], 'hy_d': hy_d[layer],
            'q_norm_g': q_norm_g[layer], 'k_norm_g': k_norm_g[layer],
            'w_branch_a': w_branch_a[layer].astype(BF16), 'w_branch_b': w_branch_b[layer].astype(BF16),
            'w_branch_c': w_branch_c[layer].astype(BF16), 'w_o': w_o[layer].astype(BF16),
            'norm2_g': norm2_g[layer], 'w_router': w_router[layer], 'b_router': b_router[layer],
            'w_mlp1': w_mlp1[layer].astype(BF16), 'b_mlp1': b_mlp1[layer],
            'w_mlp2': w_mlp2[layer].astype(BF16), 'b_mlp2': b_mlp2[layer],
        }
        kspecs = {}
        off = 0
        for gi, (x, _) in enumerate(groups):
            l = x.shape[1]
            if l not in kspecs:
                kspecs[l] = hyena_filter_spectrum(l, p)
            mod = mod_all[layer, off:off + nb[gi]].reshape(nb[gi], N_MOD, d)
            off += nb[gi]
            cos, sin = tables[l]
            x1 = token_mixer_block(ys[gi], mod, p, cos, sin, kspecs[l])
            ys[gi] = moe_block(x1, mod, p)
    return tuple(ys)
```

```python
import functools
import math

import numpy as np
import jax
import jax.numpy as jnp
from jax import lax
from jax.experimental import pallas as pl
from jax.experimental.pallas import tpu as pltpu

F32 = jnp.float32
BF16 = jnp.bfloat16
I32 = jnp.int32

D_MODEL = 1024
N_MOD = 6
EPS = 1e-6
POOL_WINDOWS = (2, 4, 8, 16)
POOL_GROUP = 96
POOL_WIDTH = 384
POOL_HALO = 16
HYENA_WIDTH = 384
FILTER_ORDER = 64
FILTER_BANDS = 16
HYENA_FAST_DECAY = 0.3
HYENA_SLOW_DECAY = 1.5
HYENA_TARGET = 1e-2
HEAD_DIM = 64
ATTN_GROUPS = ((128, 1), (512, 4), (2048, 16))
HEADS_PER_GROUP = 4
ATTN_WIDTH = 768
ATTN_OUT = 256
ATTN_HALF = 64
ATTN_QUERY_TILE = 128
ATTN_STEP_ROWS = 512
ROPE_THETA = 10000.0
NEG_BIG = -1e30
OFF_HYENA = POOL_WIDTH
OFF_ATTN = OFF_HYENA + 3 * HYENA_WIDTH
OFF_GATE = OFF_ATTN + 3 * ATTN_WIDTH
IN_COLS = OFF_GATE + 3 * D_MODEL
N_EXPERTS = 32
TOP_K = 4
SWIGLU_LIMIT = 7.0
SWIGLU_ALPHA = 1.702
LANES = 128
MOE_ROWS = 256
VMEM_LIMIT = 56 * 1024 * 1024


def _cp(sem, vmem=None):
    return pltpu.CompilerParams(dimension_semantics=sem, vmem_limit_bytes=vmem)


def _split(x):
    hi = x.astype(BF16)
    lo = (x - hi.astype(F32)).astype(BF16)
    return hi, lo


def _dot(a, b):
    return jnp.dot(a, b, preferred_element_type=F32)


def _dot3(a_hi, a_lo, b):
    b_hi, b_lo = _split(b)
    return _dot(a_hi, b_hi) + _dot(a_hi, b_lo) + _dot(a_lo, b_hi)


def _dot3r(a, b_hi, b_lo):
    a_hi, a_lo = _split(a)
    return _dot(a_hi, b_hi) + _dot(a_lo, b_hi) + _dot(a_hi, b_lo)


def _fft_factors(n):
    n2 = 128 if n >= 16384 else 64
    return n // n2, n2


def _ada_kernel(c_ref, w_ref, b_ref, o_ref):
    c = c_ref[...]
    s = c * jax.nn.sigmoid(c)
    w_hi, w_lo = _split(w_ref[0])
    o_ref[0] = _dot3r(s, w_hi, w_lo) + b_ref[0]


def ada_modulation(c_all, ada_w, ada_b):
    nl, d, n = ada_w.shape
    bp = c_all.shape[0]
    tn = 1536
    return pl.pallas_call(
        _ada_kernel,
        out_shape=jax.ShapeDtypeStruct((nl, bp, n), F32),
        grid=(nl, n // tn),
        in_specs=[pl.BlockSpec((bp, d), lambda l, j: (0, 0)),
                  pl.BlockSpec((1, d, tn), lambda l, j: (l, 0, j)),
                  pl.BlockSpec((1, 1, tn), lambda l, j: (l, 0, j))],
        out_specs=pl.BlockSpec((1, bp, tn), lambda l, j: (l, 0, j)),
        compiler_params=_cp(("parallel", "parallel"), VMEM_LIMIT),
        name="ada_modulation",
    )(c_all, ada_w, ada_b.reshape(nl, 1, n))


IN_SPLITS = ((0, POOL_WIDTH), (OFF_HYENA, 3 * HYENA_WIDTH), (OFF_ATTN, 3 * ATTN_WIDTH),
             (OFF_GATE, 3 * D_MODEL))
IN_CHUNK = 768


def _modulated_norm(x, g, scale, shift):
    ms = jnp.mean(x * x, axis=-1, keepdims=True)
    return (x * lax.rsqrt(ms + EPS) * g) * (1.0 + scale) + shift


def _inproj_kernel(x_ref, mod_ref, g_ref, w_ref, b_ref, a_ref, hy_ref, qkv_ref, gt_ref):
    h = _modulated_norm(x_ref[0], g_ref[...], mod_ref[0, 1:2, :], mod_ref[0, 0:1, :])
    hb = h.astype(BF16)
    for (off, width), o_ref in zip(IN_SPLITS, (a_ref, hy_ref, qkv_ref, gt_ref)):
        for c0 in range(0, width, IN_CHUNK):
            c1 = min(c0 + IN_CHUNK, width)
            o_ref[0, :, c0:c1] = (_dot(hb, w_ref[:, off + c0:off + c1])
                                  + b_ref[:, off + c0:off + c1])


def in_projection(x, mod, g1, w_in_bf, b_in):
    b, l, d = x.shape
    tm = 256
    outs = tuple(jax.ShapeDtypeStruct((b, l, w), F32) for _, w in IN_SPLITS)
    return pl.pallas_call(
        _inproj_kernel,
        out_shape=outs,
        grid=(b, l // tm),
        in_specs=[pl.BlockSpec((1, tm, d), lambda bi, i: (bi, i, 0)),
                  pl.BlockSpec((1, N_MOD, d), lambda bi, i: (bi, 0, 0)),
                  pl.BlockSpec((1, d), lambda bi, i: (0, 0)),
                  pl.BlockSpec((d, IN_COLS), lambda bi, i: (0, 0), pipeline_mode=pl.Buffered(1)),
                  pl.BlockSpec((1, IN_COLS), lambda bi, i: (0, 0))],
        out_specs=tuple(pl.BlockSpec((1, tm, w), lambda bi, i: (bi, i, 0)) for _, w in IN_SPLITS),
        compiler_params=_cp(("parallel", "parallel"), VMEM_LIMIT),
        name="in_projection",
    )(x, mod, g1.reshape(1, d), w_in_bf, b_in.reshape(1, IN_COLS))


def _hyena_pre_kernel(cur_ref, prev_ref, next_ref, w_ref, b_ref, x0_ref, z_ref):
    i = pl.program_id(1)
    last = pl.num_programs(1) - 1
    cur = cur_ref[0]
    tm = cur.shape[0]
    prev_row = jnp.where(i > 0, prev_ref[0, 7:8, :], 0.0)
    next_row = jnp.where(i < last, next_ref[0, 0:1, :], 0.0)
    row = lax.broadcasted_iota(I32, (tm, 1), 0)
    up = jnp.where(row == 0, prev_row, pltpu.roll(cur, 1, 0))
    dn = jnp.where(row == tm - 1, next_row, pltpu.roll(cur, tm - 1, 0))
    y = w_ref[0:1, :] * up + w_ref[1:2, :] * cur + w_ref[2:3, :] * dn + b_ref[...]
    c = HYENA_WIDTH
    x0_ref[0] = y[:, :c]
    z_ref[0] = y[:, c:2 * c] * y[:, 2 * c:]


def hyena_pre(hy, conv_w, conv_b):
    b, l, w = hy.shape
    tm = 512
    nb8 = l // 8
    sds = jax.ShapeDtypeStruct((b, l, HYENA_WIDTH), F32)
    return pl.pallas_call(
        _hyena_pre_kernel,
        out_shape=(sds, sds),
        grid=(b, l // tm),
        in_specs=[pl.BlockSpec((1, tm, w), lambda bi, i: (bi, i, 0)),
                  pl.BlockSpec((1, 8, w), lambda bi, i: (bi, jnp.maximum(i * (tm // 8) - 1, 0), 0)),
                  pl.BlockSpec((1, 8, w), lambda bi, i: (bi, jnp.minimum((i + 1) * (tm // 8), nb8 - 1), 0)),
                  pl.BlockSpec((3, w), lambda bi, i: (0, 0)),
                  pl.BlockSpec((1, w), lambda bi, i: (0, 0))],
        out_specs=(pl.BlockSpec((1, tm, HYENA_WIDTH), lambda bi, i: (bi, i, 0)),
                   pl.BlockSpec((1, tm, HYENA_WIDTH), lambda bi, i: (bi, i, 0))),
        compiler_params=_cp(("parallel", "parallel"), VMEM_LIMIT),
        name="hyena_pre",
    )(hy, hy, hy, conv_w, conv_b.reshape(1, w))


def _filter_kernel(w1_ref, b1_ref, w2_ref, b2_ref, w3_ref, b3_ref, w4_ref, fq_ref,
                   k_ref, asum_ref, *, seq):
    i = pl.program_id(0)
    tr = k_ref.shape[0]
    c = HYENA_WIDTH
    m = i * tr + lax.broadcasted_iota(I32, (tr, 1), 0)
    t_idx = jnp.where(m < seq, m, 2 * seq - m)
    tf = t_idx.astype(F32)
    t = tf * (1.0 / (seq - 1))
    omega = (2.0 * math.pi / seq) * tf
    lane = lax.broadcasted_iota(I32, (tr, LANES), 1)
    band = (lane % FILTER_BANDS).astype(F32)
    fr = 1e-4 + band * ((FILTER_BANDS - 1 - 1e-4) / (FILTER_BANDS - 1))
    arg = fr * omega
    feat = jnp.where(lane < FILTER_BANDS, jnp.cos(arg),
                     jnp.where(lane < 2 * FILTER_BANDS, -jnp.sin(arg),
                               jnp.where(lane == 2 * FILTER_BANDS, t, 0.0)))
    fq = fq_ref[...]

    def layer(h, w_ref_, b_ref_):
        w_hi, w_lo = _split(w_ref_[...])
        return jnp.sin(fq * (_dot3r(h, w_hi, w_lo) + b_ref_[...]))

    h = layer(feat, w1_ref, b1_ref)
    h = layer(h, w2_ref, b2_ref)
    h = layer(h, w3_ref, b3_ref)
    w4_hi, w4_lo = _split(w4_ref[...])
    h = _dot3r(h, w4_hi, w4_lo)
    ch = lax.broadcasted_iota(I32, (1, c), 1).astype(F32)
    max_decay = math.log(HYENA_TARGET) / HYENA_FAST_DECAY
    min_decay = math.log(HYENA_TARGET) / HYENA_SLOW_DECAY
    delta = jnp.abs(min_decay + ch * ((max_decay - min_decay) / (c - 1)))
    decay = jnp.exp(-t * delta)
    k = jnp.where(m < seq, h[:, :c], jnp.where(m > seq, h[:, c:], 0.0)) * decay
    k_ref[...] = k

    @pl.when(i == 0)
    def _():
        asum_ref[...] = jnp.zeros_like(asum_ref)

    asum_ref[...] += jnp.sum(jnp.abs(k), axis=0, keepdims=True)


def hyena_filter_taps(seq, w1, b1, w2, b2, w3, b3, w4, fq):
    c = HYENA_WIDTH
    tr = 512
    fo = FILTER_ORDER
    w1p = jnp.concatenate([w1[1:1 + 2 * FILTER_BANDS], w1[0:1],
                           jnp.zeros((LANES - 2 * FILTER_BANDS - 1, fo), F32)], axis=0)
    full = lambda shape: pl.BlockSpec(shape, lambda i: (0, 0))
    return pl.pallas_call(
        functools.partial(_filter_kernel, seq=seq),
        out_shape=(jax.ShapeDtypeStruct((2 * seq, c), F32), jax.ShapeDtypeStruct((1, c), F32)),
        grid=(2 * seq // tr,),
        in_specs=[full((LANES, fo)), full((1, fo)), full((fo, fo)), full((1, fo)),
                  full((fo, fo)), full((1, fo)), full((fo, 2 * c)), full((1, fo))],
        out_specs=(pl.BlockSpec((tr, c), lambda i: (i, 0)), full((1, c))),
        compiler_params=_cp(("arbitrary",), VMEM_LIMIT),
        name="hyena_filter",
    )(w1p, b1.reshape(1, fo), w2, b2.reshape(1, fo), w3, b3.reshape(1, fo),
      w4, fq.reshape(1, fo))


def _dft_mats(n):
    k = np.arange(n)
    ang = -2.0 * np.pi * np.outer(k, k) / n
    return np.cos(ang), np.sin(ang)


def _split_np(m):
    m32 = jnp.asarray(m, F32)
    return _split(m32)


def _stage_outer_mats(n1):
    fr, fi = _dft_mats(n1)
    h = n1 // 2
    fwd_pair = np.block([[fr[:, :h], -fi[:, :h]], [fi[:, :h], fr[:, :h]]])
    fwd_real = np.concatenate([fr, fi], axis=0)
    inv_pair = np.block([[fr[:h, :], fi[:h, :]], [-fi[:h, :], fr[:h, :]]])
    return _split_np(fwd_pair), _split_np(fwd_real), _split_np(inv_pair)


def _stage_inner_mats(n2):
    fr, fi = _dft_mats(n2)
    fwd = np.block([[fr, -fi], [fi, fr]])
    inv = np.block([[fr, fi], [-fi, fr]])
    return _split_np(fwd), _split_np(inv)


def _left_dft_kernel(mh_ref, ml_ref, x_ref, o_ref):
    o_ref[0] = _dot3(mh_ref[...], ml_ref[...], x_ref[0])


def left_dft(mats, x, tc=1536):
    m_hi, m_lo = mats
    r, k = m_hi.shape
    p, _, cols = x.shape
    return pl.pallas_call(
        _left_dft_kernel,
        out_shape=jax.ShapeDtypeStruct((p, r, cols), F32),
        grid=(p, cols // tc),
        in_specs=[pl.BlockSpec((r, k), lambda pi, j: (0, 0)),
                  pl.BlockSpec((r, k), lambda pi, j: (0, 0)),
                  pl.BlockSpec((1, k, tc), lambda pi, j: (pi, 0, j))],
        out_specs=pl.BlockSpec((1, r, tc), lambda pi, j: (pi, 0, j)),
        compiler_params=_cp(("parallel", "parallel"), VMEM_LIMIT),
        name="hyena_dft_outer",
    )(m_hi, m_lo, x)


def _left_idft_kernel(mh_ref, ml_ref, y_ref, x0_ref, z_ref, d_ref, o_ref):
    conv = _dot3(mh_ref[...], ml_ref[...], y_ref[0])
    z = z_ref[0]
    o_ref[0] = x0_ref[0] * (conv + z * d_ref[...])


def left_idft_gate(mats, spec, x0, z, d_tiled, tc=1536):
    m_hi, m_lo = mats
    r, k = m_hi.shape
    p, _, cols = spec.shape
    return pl.pallas_call(
        _left_idft_kernel,
        out_shape=jax.ShapeDtypeStruct((p, r, cols), F32),
        grid=(p, cols // tc),
        in_specs=[pl.BlockSpec((r, k), lambda pi, j: (0, 0)),
                  pl.BlockSpec((r, k), lambda pi, j: (0, 0)),
                  pl.BlockSpec((1, k, tc), lambda pi, j: (pi, 0, j)),
                  pl.BlockSpec((1, r, tc), lambda pi, j: (pi, 0, j)),
                  pl.BlockSpec((1, r, tc), lambda pi, j: (pi, 0, j)),
                  pl.BlockSpec((1, tc), lambda pi, j: (0, j))],
        out_specs=pl.BlockSpec((1, r, tc), lambda pi, j: (pi, 0, j)),
        compiler_params=_cp(("parallel", "parallel"), VMEM_LIMIT),
        name="hyena_idft_outer",
    )(m_hi, m_lo, spec, x0, z, d_tiled)


def _twiddle(k1, n2, n):
    row = lax.broadcasted_iota(I32, (n2, LANES), 0)
    ang = ((row * k1) & (n - 1)).astype(F32) * (2.0 * math.pi / n)
    reps = HYENA_WIDTH // LANES
    c = jnp.concatenate([jnp.cos(ang)] * reps, axis=1)
    s = jnp.concatenate([jnp.sin(ang)] * reps, axis=1)
    return c, s


def _spec_filter_kernel(fh_ref, fl_ref, a_ref, scale_ref, o_ref, *, n):
    n2 = a_ref.shape[3]
    c, s = _twiddle(pl.program_id(0), n2, n)
    ar, ai = a_ref[0, 0, 0], a_ref[0, 1, 0]
    x = jnp.concatenate([ar * c + ai * s, ai * c - ar * s], axis=0)
    y = _dot3(fh_ref[...], fl_ref[...], x) * scale_ref[...]
    o_ref[0, 0] = y[:n2]
    o_ref[1, 0] = y[n2:]


def filter_spectrum(inner_fwd, a, scale, n):
    _, _, n1, n2, c = a.shape
    fh, fl = inner_fwd
    return pl.pallas_call(
        functools.partial(_spec_filter_kernel, n=n),
        out_shape=jax.ShapeDtypeStruct((2, n1, n2, c), F32),
        grid=(n1,),
        in_specs=[pl.BlockSpec((2 * n2, 2 * n2), lambda k: (0, 0)),
                  pl.BlockSpec((2 * n2, 2 * n2), lambda k: (0, 0)),
                  pl.BlockSpec((1, 2, 1, n2, c), lambda k: (0, 0, k, 0, 0)),
                  pl.BlockSpec((1, c), lambda k: (0, 0))],
        out_specs=pl.BlockSpec((2, 1, n2, c), lambda k: (0, k, 0, 0)),
        compiler_params=_cp(("parallel",), VMEM_LIMIT),
        name="hyena_filter_spectrum",
    )(fh, fl, a, scale)


def _spec_conv_kernel(fh_ref, fl_ref, ih_ref, il_ref, a_ref, k_ref, o_ref, *, n):
    n2 = a_ref.shape[3]
    c, s = _twiddle(pl.program_id(1), n2, n)
    ar, ai = a_ref[0, 0, 0], a_ref[0, 1, 0]
    x = jnp.concatenate([ar * c + ai * s, ai * c - ar * s], axis=0)
    y = _dot3(fh_ref[...], fl_ref[...], x)
    yr, yi = y[:n2], y[n2:]
    kr, ki = k_ref[0, 0], k_ref[1, 0]
    p = jnp.concatenate([yr * kr - yi * ki, yr * ki + yi * kr], axis=0)
    w = _dot3(ih_ref[...], il_ref[...], p)
    wr, wi = w[:n2], w[n2:]
    o_ref[0, 0, 0] = wr * c - wi * s
    o_ref[0, 1, 0] = wr * s + wi * c


def spectrum_convolve(inner, a, kspec, n):
    p, _, n1, n2, c = a.shape
    (fh, fl), (ih, il) = inner
    msp = pl.BlockSpec((2 * n2, 2 * n2), lambda pi, k: (0, 0))
    return pl.pallas_call(
        functools.partial(_spec_conv_kernel, n=n),
        out_shape=jax.ShapeDtypeStruct((p, 2, n1, n2, c), F32),
        grid=(p, n1),
        in_specs=[msp, msp, msp, msp,
                  pl.BlockSpec((1, 2, 1, n2, c), lambda pi, k: (pi, 0, k, 0, 0)),
                  pl.BlockSpec((2, 1, n2, c), lambda pi, k: (0, k, 0, 0))],
        out_specs=pl.BlockSpec((1, 2, 1, n2, c), lambda pi, k: (pi, 0, k, 0, 0)),
        compiler_params=_cp(("parallel", "parallel"), VMEM_LIMIT),
        name="hyena_spectrum_convolve",
    )(fh, fl, ih, il, a, kspec)


def hyena_filter_spectrum(seq, p):
    n = 2 * seq
    n1, n2 = _fft_factors(n)
    c = HYENA_WIDTH
    taps, asum = hyena_filter_taps(seq, p['filt_w1'], p['filt_b1'], p['filt_w2'], p['filt_b2'],
                                   p['filt_w3'], p['filt_b3'], p['filt_w4'], p['filt_freq'])
    _, fwd_real, _ = _stage_outer_mats(n1)
    a = left_dft(fwd_real, taps.reshape(1, n1, n2 * c))
    inner_fwd, _ = _stage_inner_mats(n2)
    scale = 1.0 / (asum * n)
    return filter_spectrum(inner_fwd, a.reshape(1, 2, n1, n2, c), scale, n)


def hyena_long_conv(x0, z, kspec, hy_d):
    b, l, c = z.shape
    n = 2 * l
    n1, n2 = _fft_factors(n)
    fwd_pair, _, inv_pair = _stage_outer_mats(n1)
    inner = _stage_inner_mats(n2)
    zp = z.reshape(b // 2, n1, n2 * c)
    a = left_dft(fwd_pair, zp)
    w = spectrum_convolve(inner, a.reshape(b // 2, 2, n1, n2, c), kspec, n)
    d_tiled = jnp.tile(hy_d.reshape(1, c), (1, n2))
    y = left_idft_gate(inv_pair, w.reshape(b // 2, 2 * n1, n2 * c),
                       x0.reshape(b // 2, n1, n2 * c), zp, d_tiled)
    return y.reshape(b, l, c)


def _rope_table_kernel(inv_ref, cos_ref, sin_ref):
    tm = cos_ref.shape[0]
    pos = (pl.program_id(0) * tm + lax.broadcasted_iota(I32, (tm, LANES), 0)).astype(F32)
    ang = pos * inv_ref[...]
    lane = lax.broadcasted_iota(I32, (tm, LANES), 1)
    cos_ref[...] = jnp.cos(ang)
    sin_ref[...] = jnp.where(lane % HEAD_DIM < HEAD_DIM // 2, -jnp.sin(ang), jnp.sin(ang))


def rope_tables(seq):
    inv = ROPE_THETA ** (-jnp.arange(0, HEAD_DIM, 2, dtype=F32) / HEAD_DIM)
    inv128 = jnp.tile(inv, LANES // (HEAD_DIM // 2)).reshape(1, LANES)
    tm = 512
    sds = jax.ShapeDtypeStruct((seq, LANES), F32)
    return pl.pallas_call(
        _rope_table_kernel,
        out_shape=(sds, sds),
        grid=(seq // tm,),
        in_specs=[pl.BlockSpec((1, LANES), lambda i: (0, 0))],
        out_specs=(pl.BlockSpec((tm, LANES), lambda i: (i, 0)), pl.BlockSpec((tm, LANES), lambda i: (i, 0))),
        compiler_params=_cp(("parallel",), VMEM_LIMIT),
        name="rope_tables",
    )(inv128)


def _norm_rope(t, g, cos, sin_signed, ones_bd, scale):
    sq = t * t
    parts = []
    for c0 in range(0, ATTN_WIDTH, 256):
        hi, lo = _split(sq[:, c0:c0 + 256])
        parts.append(_dot(hi, ones_bd) + _dot(lo, ones_bd))
    ms = jnp.concatenate(parts, axis=1) * (1.0 / HEAD_DIM)
    tn = t * lax.rsqrt(ms + EPS) * g
    lane = lax.broadcasted_iota(I32, (t.shape[0], LANES), 1)
    first_half = lane % HEAD_DIM < HEAD_DIM // 2
    outs = []
    for c0 in range(0, ATTN_WIDTH, LANES):
        x = tn[:, c0:c0 + LANES]
        rot = jnp.where(first_half, pltpu.roll(x, LANES - HEAD_DIM // 2, 1), pltpu.roll(x, HEAD_DIM // 2, 1))
        outs.append((x * cos + rot * sin_signed) * scale)
    return jnp.concatenate(outs, axis=1)


def _qk_prep_kernel(qkv_ref, cos_ref, sin_ref, qg_ref, kg_ref, bd_ref, q_ref, k_ref, v_ref):
    w = ATTN_WIDTH
    cos, sin = cos_ref[...], sin_ref[...]
    bd = bd_ref[...]
    q_ref[0] = _norm_rope(qkv_ref[0, :, :w], qg_ref[...], cos, sin, bd, HEAD_DIM ** -0.5).astype(BF16)
    k_ref[0] = _norm_rope(qkv_ref[0, :, w:2 * w], kg_ref[...], cos, sin, bd, 1.0).astype(BF16)
    v_ref[0] = qkv_ref[0, :, 2 * w:].astype(BF16)


def qk_prep(qkv, cos, sin, q_g, k_g):
    b, l, _ = qkv.shape
    tm = 512
    w = ATTN_WIDTH
    head = np.arange(256) // HEAD_DIM
    ones_bd = jnp.asarray(head[:, None] == head[None, :], BF16)
    sds = jax.ShapeDtypeStruct((b, l, w), BF16)
    tile_g = lambda g: jnp.tile(g, w // HEAD_DIM).reshape(1, w)
    ospec = pl.BlockSpec((1, tm, w), lambda bi, i: (bi, i, 0))
    return pl.pallas_call(
        _qk_prep_kernel,
        out_shape=(sds, sds, sds),
        grid=(b, l // tm),
        in_specs=[pl.BlockSpec((1, tm, 3 * w), lambda bi, i: (bi, i, 0)),
                  pl.BlockSpec((tm, LANES), lambda bi, i: (i, 0)),
                  pl.BlockSpec((tm, LANES), lambda bi, i: (i, 0)),
                  pl.BlockSpec((1, w), lambda bi, i: (0, 0)),
                  pl.BlockSpec((1, w), lambda bi, i: (0, 0)),
                  pl.BlockSpec((256, 256), lambda bi, i: (0, 0))],
        out_specs=(ospec, ospec, ospec),
        compiler_params=_cp(("parallel", "parallel"), VMEM_LIMIT),
        name="qk_norm_rope",
    )(qkv, cos, sin, tile_g(q_g), tile_g(k_g), ones_bd)


def _band_attn_kernel(q_ref, kp_ref, kc_ref, kn_ref, vp_ref, vc_ref, vn_ref, o_ref, lse_ref, *, m_len):
    i = pl.program_id(2)
    ts = q_ref.shape[1]
    h = ATTN_HALF
    tq = min(ATTN_QUERY_TILE, ts)
    kk = jnp.concatenate([kp_ref[0], kc_ref[0], kn_ref[0]], axis=0)
    vv = jnp.concatenate([vp_ref[0], vc_ref[0], vn_ref[0]], axis=0)
    qi = lax.broadcasted_iota(I32, (tq, tq + 2 * h), 0)
    kj = lax.broadcasted_iota(I32, (tq, tq + 2 * h), 1)
    in_band = jnp.abs(kj - h - qi) <= h
    for q0 in range(0, ts, tq):
        q = q_ref[0, q0:q0 + tq, :]
        kpos = i * ts + q0 - h + kj
        mask = in_band & (kpos >= 0) & (kpos < m_len)
        outs, lses = [], []
        for hd in range(HEADS_PER_GROUP):
            sl = slice(hd * HEAD_DIM, (hd + 1) * HEAD_DIM)
            s = lax.dot_general(q[:, sl], kk[q0:q0 + tq + 2 * h, sl], (((1,), (1,)), ((), ())),
                                preferred_element_type=F32)
            s = jnp.where(mask, s, NEG_BIG)
            mx = jnp.max(s, axis=1, keepdims=True)
            p = jnp.exp(s - mx)
            den = jnp.sum(p, axis=1, keepdims=True)
            o = _dot(p.astype(BF16), vv[q0:q0 + tq + 2 * h, sl]) / den
            outs.append(o)
            lses.append(jnp.broadcast_to(mx + jnp.log(den), (tq, HEAD_DIM)))
        o_ref[0, q0:q0 + tq, :] = jnp.concatenate(outs, axis=1)
        lse_ref[0, q0:q0 + tq, :] = jnp.concatenate(lses, axis=1)


def band_attention(q, k, v, group, dil):
    b, l, w = q.shape
    m_len = l // dil
    tq = min(ATTN_STEP_ROWS, m_len)
    hb = ATTN_HALF
    gw = ATTN_OUT
    nhb = m_len // hb
    view = lambda t: t.reshape(b, m_len, dil * w)
    col = lambda r: r * (w // gw) + group
    cur = pl.BlockSpec((1, tq, gw), lambda bi, r, i: (bi, i, col(r)))
    prev = pl.BlockSpec((1, hb, gw), lambda bi, r, i: (bi, jnp.maximum(i * (tq // hb) - 1, 0), col(r)))
    nxt = pl.BlockSpec((1, hb, gw), lambda bi, r, i: (bi, jnp.minimum((i + 1) * (tq // hb), nhb - 1), col(r)))
    ospec = pl.BlockSpec((1, tq, gw), lambda bi, r, i: (bi, i, r))
    sds = jax.ShapeDtypeStruct((b, m_len, dil * gw), F32)
    qv, kv, vv = view(q), view(k), view(v)
    o, lse = pl.pallas_call(
        functools.partial(_band_attn_kernel, m_len=m_len),
        out_shape=(sds, sds),
        grid=(b, dil, m_len // tq),
        in_specs=[cur, prev, cur, nxt, prev, cur, nxt],
        out_specs=(ospec, ospec),
        compiler_params=_cp(("parallel", "parallel", "parallel"), VMEM_LIMIT),
        name=f"band_attention_d{dil}",
    )(qv, kv, kv, kv, vv, vv, vv)
    return o.reshape(b, l, gw), lse.reshape(b, l, gw)


def _pooled(a_cur, a_prev, a_next, i, last, seq):
    tm = a_cur.shape[0]
    hl = POOL_HALO
    prev = jnp.where(i > 0, a_prev, 0.0)
    nxt = jnp.where(i < last, a_next, 0.0)
    ext = jnp.concatenate([prev, a_cur, nxt], axis=0)
    rows = ext.shape[0]
    lane = lax.broadcasted_iota(I32, (1, POOL_WIDTH), 1)
    pos = i * tm + lax.broadcasted_iota(I32, (tm, 1), 0)
    trail = ext
    win = None
    count = None
    for gi, w in enumerate(POOL_WINDOWS):
        trail = trail + pltpu.roll(trail, w // 2, 0)
        left, right = w // 2, w - 1 - w // 2
        centred = pltpu.roll(trail, rows - right, 0) if right else trail
        centred = centred[hl:hl + tm]
        cnt = (jnp.minimum(pos + right, seq - 1) - jnp.maximum(pos - left, 0) + 1).astype(F32)
        if win is None:
            win, count = centred, jnp.broadcast_to(cnt, (tm, POOL_WIDTH))
        else:
            sel = lane >= gi * POOL_GROUP
            win = jnp.where(sel, centred, win)
            count = jnp.where(sel, cnt, count)
    return win / count - a_cur


def _merge_kernel(a_ref, ap_ref, an_ref, yb_ref, o0_ref, o1_ref, o2_ref, l0_ref, l1_ref, l2_ref,
                  gt_ref, x_ref, mod_ref, pw_ref, ps_ref, wa_ref, wb_ref, wc_ref, wo_ref, out_ref, *, seq):
    i = pl.program_id(1)
    last = pl.num_programs(1) - 1
    d = D_MODEL
    pooled = _pooled(a_ref[0], ap_ref[0], an_ref[0], i, last, seq)
    ya = _dot(pooled.astype(BF16), pw_ref[...]) * ps_ref[...]
    l0, l1, l2 = l0_ref[0], l1_ref[0], l2_ref[0]
    mx = jnp.maximum(jnp.maximum(l0, l1), l2)
    e0, e1, e2 = jnp.exp(l0 - mx), jnp.exp(l1 - mx), jnp.exp(l2 - mx)
    yc = (e0 * o0_ref[0] + e1 * o1_ref[0] + e2 * o2_ref[0]) / (e0 + e1 + e2)
    merged = (jax.nn.sigmoid(gt_ref[0, :, :d]) * _dot(ya.astype(BF16), wa_ref[...])
              + jax.nn.sigmoid(gt_ref[0, :, d:2 * d]) * _dot(yb_ref[0].astype(BF16), wb_ref[...])
              + jax.nn.sigmoid(gt_ref[0, :, 2 * d:]) * _dot(yc.astype(BF16), wc_ref[...]))
    out_ref[0] = x_ref[0] + mod_ref[0, 2:3, :] * _dot(merged.astype(BF16), wo_ref[...])


def merge_branches(a, yb, attn, gates, x, mod, pool_bd, pool_scale, wa, wb, wc, wo):
    b, l, d = x.shape
    tm = 256
    hl = POOL_HALO
    nh = l // hl
    (o0, l0), (o1, l1), (o2, l2) = attn
    row = lambda w: pl.BlockSpec((1, tm, w), lambda bi, i: (bi, i, 0))
    full = lambda s: pl.BlockSpec(s, lambda bi, i: (0, 0))
    return pl.pallas_call(
        functools.partial(_merge_kernel, seq=l),
        out_shape=jax.ShapeDtypeStruct((b, l, d), F32),
        grid=(b, l // tm),
        in_specs=[row(POOL_WIDTH),
                  pl.BlockSpec((1, hl, POOL_WIDTH), lambda bi, i: (bi, jnp.maximum(i * (tm // hl) - 1, 0), 0)),
                  pl.BlockSpec((1, hl, POOL_WIDTH), lambda bi, i: (bi, jnp.minimum((i + 1) * (tm // hl), nh - 1), 0)),
                  row(HYENA_WIDTH), row(ATTN_OUT), row(ATTN_OUT), row(ATTN_OUT),
                  row(ATTN_OUT), row(ATTN_OUT), row(ATTN_OUT), row(3 * d), row(d),
                  pl.BlockSpec((1, N_MOD, d), lambda bi, i: (bi, 0, 0)),
                  full((POOL_WIDTH, POOL_WIDTH)), full((1, POOL_WIDTH)), full((POOL_WIDTH, d)),
                  full((HYENA_WIDTH, d)), full((ATTN_OUT, d)), full((d, d))],
        out_specs=row(d),
        compiler_params=_cp(("parallel", "parallel"), VMEM_LIMIT),
        name="merge_branches",
    )(a, a, a, yb, o0, o1, o2, l0, l1, l2, gates, x, mod, pool_bd, pool_scale, wa, wb, wc, wo)


TOKEN_TILE_ROWS = D_MODEL // LANES


def _store_token_tiles(ref, val):
    tm = val.shape[0]
    for j in range(TOKEN_TILE_ROWS):
        ref[pl.ds(j, tm, stride=TOKEN_TILE_ROWS), :] = val[:, j * LANES:(j + 1) * LANES]


def _load_token_tiles(ref):
    tm = ref.shape[0] // TOKEN_TILE_ROWS
    return jnp.concatenate([ref[pl.ds(j, tm, stride=TOKEN_TILE_ROWS), :] for j in range(TOKEN_TILE_ROWS)],
                           axis=1)


def _router_kernel(x_ref, mod_ref, g_ref, wr_ref, br_ref, tri_ref, h_ref, idx_ref, gw_ref, rank_ref,
                   cnt_ref, carry_ref):
    first = (pl.program_id(0) == 0) & (pl.program_id(1) == 0)

    @pl.when(first)
    def _():
        carry_ref[...] = jnp.zeros_like(carry_ref)

    h = _modulated_norm(x_ref[0], g_ref[...], mod_ref[0, 4:5, :], mod_ref[0, 3:4, :])
    _store_token_tiles(h_ref, h)
    w_hi, w_lo = _split(wr_ref[...])
    logits = _dot3r(h, w_hi, w_lo) + br_ref[...]
    tm, ne = logits.shape
    lane = lax.broadcasted_iota(I32, (tm, ne), 1).astype(F32)
    lane_out = lax.broadcasted_iota(I32, (tm, LANES), 1)
    work = logits
    vals, idxs = [], []
    member = jnp.zeros((tm, ne), F32)
    for _ in range(TOP_K):
        mx = jnp.max(work, axis=1, keepdims=True)
        idx = jnp.min(jnp.where(work == mx, lane, float(ne)), axis=1, keepdims=True)
        hit = lane == idx
        member = jnp.where(hit, 1.0, member)
        work = jnp.where(hit, -jnp.inf, work)
        vals.append(mx)
        idxs.append(idx)
    exps = [jnp.exp(v - vals[0]) for v in vals]
    den = exps[0] + exps[1] + exps[2] + exps[3]
    prefix = _dot(tri_ref[...], member.astype(BF16)) + carry_ref[...]
    idx_out = jnp.zeros((tm, LANES), I32)
    gw_out = jnp.zeros((tm, LANES), F32)
    rank_out = jnp.zeros((tm, LANES), I32)
    for k in range(TOP_K):
        rk = jnp.sum(jnp.where(lane == idxs[k], prefix, 0.0), axis=1, keepdims=True)
        idx_out = jnp.where(lane_out == k, idxs[k].astype(I32), idx_out)
        gw_out = jnp.where(lane_out == k, exps[k] / den, gw_out)
        rank_out = jnp.where(lane_out == k, rk.astype(I32), rank_out)
    idx_ref[0] = idx_out
    gw_ref[0] = gw_out
    rank_ref[0] = rank_out
    carry_ref[...] += jnp.sum(member, axis=0, keepdims=True)
    cnt_ref[...] = carry_ref[...]


def moe_router(x, mod, g2, w_router, b_router):
    b, l, d = x.shape
    tm = 256
    nl = l // tm
    ne = N_EXPERTS
    ttr = TOKEN_TILE_ROWS
    tri = jnp.asarray(np.tril(np.ones((tm, tm), np.float32), -1), BF16)
    row = lambda w: pl.BlockSpec((1, tm, w), lambda bi, i: (bi, i, 0))
    full = lambda s: pl.BlockSpec(s, lambda bi, i: (0, 0))
    return pl.pallas_call(
        _router_kernel,
        out_shape=(jax.ShapeDtypeStruct((b * l * ttr, LANES), F32),
                   jax.ShapeDtypeStruct((b, l, LANES), I32),
                   jax.ShapeDtypeStruct((b, l, LANES), F32),
                   jax.ShapeDtypeStruct((b, l, LANES), I32),
                   jax.ShapeDtypeStruct((1, ne), F32)),
        grid=(b, l // tm),
        in_specs=[row(d), pl.BlockSpec((1, N_MOD, d), lambda bi, i: (bi, 0, 0)), full((1, d)),
                  full((d, ne)), full((1, ne)), full((tm, tm))],
        out_specs=(pl.BlockSpec((tm * ttr, LANES), lambda bi, i: (bi * nl + i, 0)),
                   row(LANES), row(LANES), row(LANES), full((1, ne))),
        scratch_shapes=[pltpu.VMEM((1, ne), F32)],
        compiler_params=_cp(("arbitrary", "arbitrary"), VMEM_LIMIT),
        name="moe_router",
    )(x, mod, g2.reshape(1, d), w_router, b_router.reshape(1, ne), tri)


def _tile_copy(src, src_row, dst, dst_row, sem):
    n = TOKEN_TILE_ROWS
    return pltpu.make_async_copy(src.at[pl.ds(src_row, n), :], dst.at[pl.ds(dst_row, n), :], sem)


def _start_gather(h_hbm, tok_ref, xbuf, sem):
    n = TOKEN_TILE_ROWS
    for r in range(xbuf.shape[0] // n):
        _tile_copy(h_hbm, pl.multiple_of(tok_ref[0, 0, r], n), xbuf, r * n, sem).start()


def _start_scatter(obuf, slot_ref, out_hbm, sem):
    n = TOKEN_TILE_ROWS
    for r in range(obuf.shape[0] // n):
        _tile_copy(obuf, r * n, out_hbm, pl.multiple_of(slot_ref[0, 0, r], n), sem).start(priority=1)


def _wait_block(hbm, buf, sem):
    pltpu.make_async_copy(hbm.at[pl.ds(0, buf.shape[0]), :], buf, sem).wait()


def _expert_ffn(x, w1_ref, b1_ref, w2_ref, b2_ref):
    f = D_MODEL
    u = _dot(x.astype(BF16), w1_ref[0]) + b1_ref[0]
    x_glu = jnp.minimum(u[:, :f], SWIGLU_LIMIT)
    x_lin = jnp.clip(u[:, f:], -SWIGLU_LIMIT, SWIGLU_LIMIT)
    act = x_glu * jax.nn.sigmoid(SWIGLU_ALPHA * x_glu) * (x_lin + 1.0)
    return _dot(act.astype(BF16), w2_ref[0]) + b2_ref[0]


def _expert_kernel(blk_e_ref, tok_ref, tokn_ref, slot_ref, h_hbm, w1_ref, b1_ref, w2_ref, b2_ref, out_hbm,
                   xa, xb, oa, ob, gsem, ssem):
    i = pl.program_id(0)
    last = pl.num_programs(0) - 1

    @pl.when(i == 0)
    def _():
        _start_gather(h_hbm, tok_ref, xa, gsem.at[0])

    def step(par, x_cur, x_next, o_cur, o_prev):
        @pl.when((i % 2 == par) & (i >= 2))
        def _():
            _wait_block(out_hbm, o_cur, ssem.at[par])

        @pl.when(i % 2 == par)
        def _():
            _wait_block(h_hbm, x_cur, gsem.at[par])
            _start_gather(h_hbm, tokn_ref, x_next, gsem.at[1 - par])
            _store_token_tiles(o_cur, _expert_ffn(_load_token_tiles(x_cur), w1_ref, b1_ref, w2_ref, b2_ref))
            _start_scatter(o_cur, slot_ref, out_hbm, ssem.at[par])

        @pl.when((i == last) & (i % 2 == par))
        def _():
            _wait_block(h_hbm, x_next, gsem.at[1 - par])
            _wait_block(out_hbm, o_cur, ssem.at[par])
            _wait_block(out_hbm, o_prev, ssem.at[1 - par])

    step(0, xa, xb, oa, ob)
    step(1, xb, xa, ob, oa)


def moe_experts(h_flat, row_tok, row_slot, blk_e, n_slots, w1_bf, b1, w2_bf, b2):
    nblk, _, bm = row_tok.shape
    assert nblk >= 2
    ne, d, f2 = w1_bf.shape
    ttr = TOKEN_TILE_ROWS
    meta = blk_e.astype(I32)
    smem = lambda imap: pl.BlockSpec((1, 1, bm), imap, memory_space=pltpu.SMEM)
    grid_spec = pltpu.PrefetchScalarGridSpec(
        num_scalar_prefetch=1,
        grid=(nblk,),
        in_specs=[smem(lambda i, m: (i, 0, 0)),
                  smem(lambda i, m: (jnp.minimum(i + 1, nblk - 1), 0, 0)),
                  smem(lambda i, m: (i, 0, 0)),
                  pl.BlockSpec(memory_space=pl.ANY),
                  pl.BlockSpec((1, d, f2), lambda i, m: (m[i], 0, 0)),
                  pl.BlockSpec((1, 1, f2), lambda i, m: (m[i], 0, 0)),
                  pl.BlockSpec((1, f2 // 2, d), lambda i, m: (m[i], 0, 0)),
                  pl.BlockSpec((1, 1, d), lambda i, m: (m[i], 0, 0))],
        out_specs=pl.BlockSpec(memory_space=pl.ANY),
        scratch_shapes=[pltpu.VMEM((bm * ttr, LANES), F32), pltpu.VMEM((bm * ttr, LANES), F32),
                        pltpu.VMEM((bm * ttr, LANES), F32), pltpu.VMEM((bm * ttr, LANES), F32),
                        pltpu.SemaphoreType.DMA((2,)), pltpu.SemaphoreType.DMA((2,))],
    )
    return pl.pallas_call(
        _expert_kernel,
        out_shape=jax.ShapeDtypeStruct((n_slots * ttr, LANES), F32),
        grid_spec=grid_spec,
        compiler_params=pltpu.CompilerParams(dimension_semantics=("arbitrary",), vmem_limit_bytes=VMEM_LIMIT,
                                             disable_bounds_checks=True),
        name="moe_experts",
    )(meta, row_tok, row_tok, row_slot, h_flat, w1_bf, b1.reshape(ne, 1, f2), w2_bf, b2.reshape(ne, 1, d))


def _combine_kernel(x_ref, s0_ref, s1_ref, s2_ref, s3_ref, gw_ref, mod_ref, o_ref):
    gw = gw_ref[0]
    acc = gw[:, 0:1] * _load_token_tiles(s0_ref)
    for k, s_ref in enumerate((s1_ref, s2_ref, s3_ref), start=1):
        acc = acc + gw[:, k:k + 1] * _load_token_tiles(s_ref)
    o_ref[0] = x_ref[0] + mod_ref[0, 5:6, :] * acc


def moe_combine(x, slots, gate_w, mod):
    b, l, d = x.shape
    tm = 256
    nl = l // tm
    nt = b * nl
    plane = lambda k: pl.BlockSpec((tm * TOKEN_TILE_ROWS, LANES), lambda bi, i: (k * nt + bi * nl + i, 0))
    row = lambda w: pl.BlockSpec((1, tm, w), lambda bi, i: (bi, i, 0))
    return pl.pallas_call(
        _combine_kernel,
        out_shape=jax.ShapeDtypeStruct((b, l, d), F32),
        grid=(b, nl),
        in_specs=[row(d), plane(0), plane(1), plane(2), plane(3), row(LANES),
                  pl.BlockSpec((1, N_MOD, d), lambda bi, i: (bi, 0, 0))],
        out_specs=row(d),
        compiler_params=_cp(("parallel", "parallel"), VMEM_LIMIT),
        name="moe_combine",
    )(x, slots, slots, slots, slots, gate_w, mod)


def moe_block(x, mod, p):
    b, l, d = x.shape
    t = b * l
    bm = MOE_ROWS
    n_assign = t * TOP_K
    h, top_i, gate_w, rank, counts = moe_router(x, mod, p['norm2_g'], p['w_router'], p['b_router'])
    top_i = top_i.reshape(t, LANES)[:, :TOP_K]
    rank = rank.reshape(t, LANES)[:, :TOP_K]
    counts = counts.reshape(N_EXPERTS).astype(I32)
    pcounts = (counts + bm - 1) // bm * bm
    pends = jnp.cumsum(pcounts)
    pstarts = pends - pcounts
    dest = pstarts[top_i] + rank
    nblk = n_assign // bm + N_EXPERTS
    n_rows = nblk * bm
    slot_tk = (jnp.arange(TOP_K, dtype=I32)[None, :] * t + jnp.arange(t, dtype=I32)[:, None])
    row_slot = jnp.full((n_rows,), -1, I32).at[dest.reshape(-1)].set(
        slot_tk.reshape(-1), unique_indices=True, mode='promise_in_bounds')
    is_pad = row_slot < 0
    row_tok = jnp.where(is_pad, 0, row_slot % t)
    row_slot = jnp.where(is_pad, n_assign - 1 + jnp.cumsum(is_pad.astype(I32)), row_slot)
    blk_start = jnp.arange(nblk, dtype=I32) * bm
    blk_e = jnp.minimum(jnp.sum((pends[None, :] <= blk_start[:, None]).astype(I32), axis=1), N_EXPERTS - 1)
    ttr = TOKEN_TILE_ROWS
    slots = moe_experts(h, (row_tok * ttr).reshape(nblk, 1, bm), (row_slot * ttr).reshape(nblk, 1, bm), blk_e,
                        n_rows, p['w_mlp1'], p['b_mlp1'], p['w_mlp2'], p['b_mlp2'])
    return moe_combine(x, slots, gate_w, mod)


def token_mixer_block(x, mod, p, cos, sin, kspec):
    a, hy, qkv, gates = in_projection(x, mod, p['norm1_g'], p['w_in'], p['b_in'])
    x0, z = hyena_pre(hy, p['hy_conv_w'], p['hy_conv_b'])
    yb = hyena_long_conv(x0, z, kspec, p['hy_d'])
    q, k, v = qk_prep(qkv, cos, sin, p['q_norm_g'], p['k_norm_g'])
    attn = [band_attention(q, k, v, gi, dil) for gi, (_, dil) in enumerate(ATTN_GROUPS)]
    return merge_branches(a, yb, attn, gates, x, mod, p['pool_bd'], p['pool_scale'],
                          p['w_branch_a'], p['w_branch_b'], p['w_branch_c'], p['w_o'])


def _pool_block_diag(pool_w):
    g = POOL_GROUP
    bd = jnp.zeros((POOL_WIDTH, POOL_WIDTH), F32)
    for gi in range(len(POOL_WINDOWS)):
        bd = bd.at[gi * g:(gi + 1) * g, gi * g:(gi + 1) * g].set(pool_w[gi])
    return bd.astype(BF16)


def kernel(x_prompt, x_sample, c_prompt, c_sample, ada_w, ada_b, norm1_g, w_in, b_in, pool_w, pool_scale, hy_conv_w, hy_conv_b, filt_w1, filt_b1, filt_w2, filt_b2, filt_w3, filt_b3, filt_w4, filt_freq, hy_d, q_norm_g, k_norm_g, w_branch_a, w_branch_b, w_branch_c, w_o, norm2_g, w_router, b_router, w_mlp1, b_mlp1, w_mlp2, b_mlp2):
    depth = ada_w.shape[0]
    d = D_MODEL
    groups = [(x_prompt, c_prompt), (x_sample, c_sample)]
    nb = [x.shape[0] for x, _ in groups]
    c_all = jnp.concatenate([c for _, c in groups], axis=0)
    pad = (-c_all.shape[0]) % 8
    c_all = jnp.pad(c_all, ((0, pad), (0, 0)))
    mod_all = ada_modulation(c_all, ada_w, ada_b)
    tables = {}
    for x, _ in groups:
        l = x.shape[1]
        if l not in tables:
            tables[l] = rope_tables(l)
    ys = [x for x, _ in groups]
    for layer in range(depth):
        p = {
            'norm1_g': norm1_g[layer], 'w_in': w_in[layer].astype(BF16), 'b_in': b_in[layer],
            'pool_bd': _pool_block_diag(pool_w[layer]), 'pool_scale': pool_scale[layer].reshape(1, POOL_WIDTH),
            'hy_conv_w': hy_conv_w[layer], 'hy_conv_b': hy_conv_b[layer],
            'filt_w1': filt_w1[layer], 'filt_b1': filt_b1[layer], 'filt_w2': filt_w2[layer],
            'filt_b2': filt_b2[layer], 'filt_w3': filt_w3[layer], 'filt_b3': filt_b3[layer],
            'filt_w4': filt_w4[layer], 'filt_freq': filt_freq[layer], 'hy_d': hy_d[layer],
            'q_norm_g': q_norm_g[layer], 'k_norm_g': k_norm_g[layer],
            'w_branch_a': w_branch_a[layer].astype(BF16), 'w_branch_b': w_branch_b[layer].astype(BF16),
            'w_branch_c': w_branch_c[layer].astype(BF16), 'w_o': w_o[layer].astype(BF16),
            'norm2_g': norm2_g[layer], 'w_router': w_router[layer], 'b_router': b_router[layer],
            'w_mlp1': w_mlp1[layer].astype(BF16), 'b_mlp1': b_mlp1[layer],
            'w_mlp2': w_mlp2[layer].astype(BF16), 'b_mlp2': b_mlp2[layer],
        }
        kspecs = {}
        off = 0
        for gi, (x, _) in enumerate(groups):
            l = x.shape[1]
            if l not in kspecs:
                kspecs[l] = hyena_filter_spectrum(l, p)
            mod = mod_all[layer, off:off + nb[gi]].reshape(nb[gi], N_MOD, d)
            off += nb[gi]
            cos, sin = tables[l]
            x1 = token_mixer_block(ys[gi], mod, p, cos, sin, kspecs[l])
            ys[gi] = moe_block(x1, mod, p)
    return tuple(ys)
```

```python
import functools
import math

import numpy as np
import jax
import jax.numpy as jnp
from jax import lax
from jax.experimental import pallas as pl
from jax.experimental.pallas import tpu as pltpu

F32 = jnp.float32
BF16 = jnp.bfloat16
I32 = jnp.int32

D_MODEL = 1024
N_MOD = 6
EPS = 1e-6
POOL_WINDOWS = (2, 4, 8, 16)
POOL_GROUP = 96
POOL_WIDTH = 384
POOL_HALO = 16
HYENA_WIDTH = 384
FILTER_ORDER = 64
FILTER_BANDS = 16
HYENA_FAST_DECAY = 0.3
HYENA_SLOW_DECAY = 1.5
HYENA_TARGET = 1e-2
HEAD_DIM = 64
ATTN_GROUPS = ((128, 1), (512, 4), (2048, 16))
HEADS_PER_GROUP = 4
ATTN_WIDTH = 768
ATTN_OUT = 256
ATTN_HALF = 64
ATTN_QUERY_TILE = 128
ATTN_STEP_ROWS = 512
ROPE_THETA = 10000.0
NEG_BIG = -1e30
OFF_HYENA = POOL_WIDTH
OFF_ATTN = OFF_HYENA + 3 * HYENA_WIDTH
OFF_GATE = OFF_ATTN + 3 * ATTN_WIDTH
IN_COLS = OFF_GATE + 3 * D_MODEL
N_EXPERTS = 32
TOP_K = 4
SWIGLU_LIMIT = 7.0
SWIGLU_ALPHA = 1.702
LANES = 128
MOE_ROWS = 256
VMEM_LIMIT = 56 * 1024 * 1024


def _cp(sem, vmem=None):
    return pltpu.CompilerParams(dimension_semantics=sem, vmem_limit_bytes=vmem)


def _split(x):
    hi = x.astype(BF16)
    lo = (x - hi.astype(F32)).astype(BF16)
    return hi, lo


def _dot(a, b):
    return jnp.dot(a, b, preferred_element_type=F32)


def _dot3(a_hi, a_lo, b):
    b_hi, b_lo = _split(b)
    return _dot(a_hi, b_hi) + _dot(a_hi, b_lo) + _dot(a_lo, b_hi)


def _dot3r(a, b_hi, b_lo):
    a_hi, a_lo = _split(a)
    return _dot(a_hi, b_hi) + _dot(a_lo, b_hi) + _dot(a_hi, b_lo)


def _fft_factors(n):
    n2 = 128 if n >= 16384 else 64
    return n // n2, n2


def _ada_kernel(c_ref, w_ref, b_ref, o_ref):
    c = c_ref[...]
    s = c * jax.nn.sigmoid(c)
    w_hi, w_lo = _split(w_ref[0])
    o_ref[0] = _dot3r(s, w_hi, w_lo) + b_ref[0]


def ada_modulation(c_all, ada_w, ada_b):
    nl, d, n = ada_w.shape
    bp = c_all.shape[0]
    tn = 1536
    return pl.pallas_call(
        _ada_kernel,
        out_shape=jax.ShapeDtypeStruct((nl, bp, n), F32),
        grid=(nl, n // tn),
        in_specs=[pl.BlockSpec((bp, d), lambda l, j: (0, 0)),
                  pl.BlockSpec((1, d, tn), lambda l, j: (l, 0, j)),
                  pl.BlockSpec((1, 1, tn), lambda l, j: (l, 0, j))],
        out_specs=pl.BlockSpec((1, bp, tn), lambda l, j: (l, 0, j)),
        compiler_params=_cp(("parallel", "parallel"), VMEM_LIMIT),
        name="ada_modulation",
    )(c_all, ada_w, ada_b.reshape(nl, 1, n))


IN_SPLITS = ((0, POOL_WIDTH), (OFF_HYENA, 3 * HYENA_WIDTH), (OFF_ATTN, 3 * ATTN_WIDTH),
             (OFF_GATE, 3 * D_MODEL))
IN_CHUNK = 768


def _modulated_norm(x, g, scale, shift):
    ms = jnp.mean(x * x, axis=-1, keepdims=True)
    return (x * lax.rsqrt(ms + EPS) * g) * (1.0 + scale) + shift


def _inproj_kernel(x_ref, mod_ref, g_ref, w_ref, b_ref, a_ref, hy_ref, qkv_ref, gt_ref):
    h = _modulated_norm(x_ref[0], g_ref[...], mod_ref[0, 1:2, :], mod_ref[0, 0:1, :])
    hb = h.astype(BF16)
    for (off, width), o_ref in zip(IN_SPLITS, (a_ref, hy_ref, qkv_ref, gt_ref)):
        for c0 in range(0, width, IN_CHUNK):
            c1 = min(c0 + IN_CHUNK, width)
            o_ref[0, :, c0:c1] = (_dot(hb, w_ref[:, off + c0:off + c1])
                                  + b_ref[:, off + c0:off + c1]).astype(o_ref.dtype)


def in_projection(x, mod, g1, w_in_bf, b_in):
    b, l, d = x.shape
    tm = 256
    outs = tuple(jax.ShapeDtypeStruct((b, l, w), BF16 if off == OFF_GATE else F32) for off, w in IN_SPLITS)
    return pl.pallas_call(
        _inproj_kernel,
        out_shape=outs,
        grid=(b, l // tm),
        in_specs=[pl.BlockSpec((1, tm, d), lambda bi, i: (bi, i, 0)),
                  pl.BlockSpec((1, N_MOD, d), lambda bi, i: (bi, 0, 0)),
                  pl.BlockSpec((1, d), lambda bi, i: (0, 0)),
                  pl.BlockSpec((d, IN_COLS), lambda bi, i: (0, 0), pipeline_mode=pl.Buffered(1)),
                  pl.BlockSpec((1, IN_COLS), lambda bi, i: (0, 0))],
        out_specs=tuple(pl.BlockSpec((1, tm, w), lambda bi, i: (bi, i, 0)) for _, w in IN_SPLITS),
        compiler_params=_cp(("parallel", "parallel"), VMEM_LIMIT),
        name="in_projection",
    )(x, mod, g1.reshape(1, d), w_in_bf, b_in.reshape(1, IN_COLS))


def _hyena_pre_kernel(cur_ref, prev_ref, next_ref, w_ref, b_ref, x0_ref, z_ref):
    i = pl.program_id(1)
    last = pl.num_programs(1) - 1
    cur = cur_ref[0]
    tm = cur.shape[0]
    prev_row = jnp.where(i > 0, prev_ref[0, 7:8, :], 0.0)
    next_row = jnp.where(i < last, next_ref[0, 0:1, :], 0.0)
    row = lax.broadcasted_iota(I32, (tm, 1), 0)
    up = jnp.where(row == 0, prev_row, pltpu.roll(cur, 1, 0))
    dn = jnp.where(row == tm - 1, next_row, pltpu.roll(cur, tm - 1, 0))
    y = w_ref[0:1, :] * up + w_ref[1:2, :] * cur + w_ref[2:3, :] * dn + b_ref[...]
    c = HYENA_WIDTH
    x0_ref[0] = y[:, :c]
    z_ref[0] = y[:, c:2 * c] * y[:, 2 * c:]


def hyena_pre(hy, conv_w, conv_b):
    b, l, w = hy.shape
    tm = 512
    nb8 = l // 8
    sds = jax.ShapeDtypeStruct((b, l, HYENA_WIDTH), F32)
    return pl.pallas_call(
        _hyena_pre_kernel,
        out_shape=(sds, sds),
        grid=(b, l // tm),
        in_specs=[pl.BlockSpec((1, tm, w), lambda bi, i: (bi, i, 0)),
                  pl.BlockSpec((1, 8, w), lambda bi, i: (bi, jnp.maximum(i * (tm // 8) - 1, 0), 0)),
                  pl.BlockSpec((1, 8, w), lambda bi, i: (bi, jnp.minimum((i + 1) * (tm // 8), nb8 - 1), 0)),
                  pl.BlockSpec((3, w), lambda bi, i: (0, 0)),
                  pl.BlockSpec((1, w), lambda bi, i: (0, 0))],
        out_specs=(pl.BlockSpec((1, tm, HYENA_WIDTH), lambda bi, i: (bi, i, 0)),
                   pl.BlockSpec((1, tm, HYENA_WIDTH), lambda bi, i: (bi, i, 0))),
        compiler_params=_cp(("parallel", "parallel"), VMEM_LIMIT),
        name="hyena_pre",
    )(hy, hy, hy, conv_w, conv_b.reshape(1, w))


def _filter_kernel(w1_ref, b1_ref, w2_ref, b2_ref, w3_ref, b3_ref, w4_ref, fq_ref,
                   k_ref, asum_ref, *, seq):
    i = pl.program_id(0)
    tr = k_ref.shape[0]
    c = HYENA_WIDTH
    m = i * tr + lax.broadcasted_iota(I32, (tr, 1), 0)
    t_idx = jnp.where(m < seq, m, 2 * seq - m)
    tf = t_idx.astype(F32)
    t = tf * (1.0 / (seq - 1))
    omega = (2.0 * math.pi / seq) * tf
    lane = lax.broadcasted_iota(I32, (tr, LANES), 1)
    band = (lane % FILTER_BANDS).astype(F32)
    fr = 1e-4 + band * ((FILTER_BANDS - 1 - 1e-4) / (FILTER_BANDS - 1))
    arg = fr * omega
    feat = jnp.where(lane < FILTER_BANDS, jnp.cos(arg),
                     jnp.where(lane < 2 * FILTER_BANDS, -jnp.sin(arg),
                               jnp.where(lane == 2 * FILTER_BANDS, t, 0.0)))
    fq = fq_ref[...]

    def layer(h, w_ref_, b_ref_):
        w_hi, w_lo = _split(w_ref_[...])
        return jnp.sin(fq * (_dot3r(h, w_hi, w_lo) + b_ref_[...]))

    h = layer(feat, w1_ref, b1_ref)
    h = layer(h, w2_ref, b2_ref)
    h = layer(h, w3_ref, b3_ref)
    w4_hi, w4_lo = _split(w4_ref[...])
    h = _dot3r(h, w4_hi, w4_lo)
    ch = lax.broadcasted_iota(I32, (1, c), 1).astype(F32)
    max_decay = math.log(HYENA_TARGET) / HYENA_FAST_DECAY
    min_decay = math.log(HYENA_TARGET) / HYENA_SLOW_DECAY
    delta = jnp.abs(min_decay + ch * ((max_decay - min_decay) / (c - 1)))
    decay = jnp.exp(-t * delta)
    k = jnp.where(m < seq, h[:, :c], jnp.where(m > seq, h[:, c:], 0.0)) * decay
    k_ref[...] = k

    @pl.when(i == 0)
    def _():
        asum_ref[...] = jnp.zeros_like(asum_ref)

    asum_ref[...] += jnp.sum(jnp.abs(k), axis=0, keepdims=True)


def hyena_filter_taps(seq, w1, b1, w2, b2, w3, b3, w4, fq):
    c = HYENA_WIDTH
    tr = 512
    fo = FILTER_ORDER
    w1p = jnp.concatenate([w1[1:1 + 2 * FILTER_BANDS], w1[0:1],
                           jnp.zeros((LANES - 2 * FILTER_BANDS - 1, fo), F32)], axis=0)
    full = lambda shape: pl.BlockSpec(shape, lambda i: (0, 0))
    return pl.pallas_call(
        functools.partial(_filter_kernel, seq=seq),
        out_shape=(jax.ShapeDtypeStruct((2 * seq, c), F32), jax.ShapeDtypeStruct((1, c), F32)),
        grid=(2 * seq // tr,),
        in_specs=[full((LANES, fo)), full((1, fo)), full((fo, fo)), full((1, fo)),
                  full((fo, fo)), full((1, fo)), full((fo, 2 * c)), full((1, fo))],
        out_specs=(pl.BlockSpec((tr, c), lambda i: (i, 0)), full((1, c))),
        compiler_params=_cp(("arbitrary",), VMEM_LIMIT),
        name="hyena_filter",
    )(w1p, b1.reshape(1, fo), w2, b2.reshape(1, fo), w3, b3.reshape(1, fo),
      w4, fq.reshape(1, fo))


def _dft_mats(n):
    k = np.arange(n)
    ang = -2.0 * np.pi * np.outer(k, k) / n
    return np.cos(ang), np.sin(ang)


def _split_np(m):
    m32 = jnp.asarray(m, F32)
    return _split(m32)


def _stage_outer_mats(n1):
    fr, fi = _dft_mats(n1)
    h = n1 // 2
    fwd_pair = np.block([[fr[:, :h], -fi[:, :h]], [fi[:, :h], fr[:, :h]]])
    fwd_real = np.concatenate([fr, fi], axis=0)
    inv_pair = np.block([[fr[:h, :], fi[:h, :]], [-fi[:h, :], fr[:h, :]]])
    return _split_np(fwd_pair), _split_np(fwd_real), _split_np(inv_pair)


def _stage_inner_mats(n2):
    fr, fi = _dft_mats(n2)
    fwd = np.block([[fr, -fi], [fi, fr]])
    inv = np.block([[fr, fi], [-fi, fr]])
    return _split_np(fwd), _split_np(inv)


def _left_dft_kernel(mh_ref, ml_ref, x_ref, o_ref):
    o_ref[0] = _dot3(mh_ref[...], ml_ref[...], x_ref[0])


def left_dft(mats, x, tc=1536):
    m_hi, m_lo = mats
    r, k = m_hi.shape
    p, _, cols = x.shape
    return pl.pallas_call(
        _left_dft_kernel,
        out_shape=jax.ShapeDtypeStruct((p, r, cols), F32),
        grid=(p, cols // tc),
        in_specs=[pl.BlockSpec((r, k), lambda pi, j: (0, 0)),
                  pl.BlockSpec((r, k), lambda pi, j: (0, 0)),
                  pl.BlockSpec((1, k, tc), lambda pi, j: (pi, 0, j))],
        out_specs=pl.BlockSpec((1, r, tc), lambda pi, j: (pi, 0, j)),
        compiler_params=_cp(("parallel", "parallel"), VMEM_LIMIT),
        name="hyena_dft_outer",
    )(m_hi, m_lo, x)


def _left_idft_kernel(mh_ref, ml_ref, y_ref, x0_ref, z_ref, d_ref, o_ref):
    conv = _dot3(mh_ref[...], ml_ref[...], y_ref[0])
    z = z_ref[0]
    o_ref[0] = x0_ref[0] * (conv + z * d_ref[...])


def left_idft_gate(mats, spec, x0, z, d_tiled, tc=1536):
    m_hi, m_lo = mats
    r, k = m_hi.shape
    p, _, cols = spec.shape
    return pl.pallas_call(
        _left_idft_kernel,
        out_shape=jax.ShapeDtypeStruct((p, r, cols), F32),
        grid=(p, cols // tc),
        in_specs=[pl.BlockSpec((r, k), lambda pi, j: (0, 0)),
                  pl.BlockSpec((r, k), lambda pi, j: (0, 0)),
                  pl.BlockSpec((1, k, tc), lambda pi, j: (pi, 0, j)),
                  pl.BlockSpec((1, r, tc), lambda pi, j: (pi, 0, j)),
                  pl.BlockSpec((1, r, tc), lambda pi, j: (pi, 0, j)),
                  pl.BlockSpec((1, tc), lambda pi, j: (0, j))],
        out_specs=pl.BlockSpec((1, r, tc), lambda pi, j: (pi, 0, j)),
        compiler_params=_cp(("parallel", "parallel"), VMEM_LIMIT),
        name="hyena_idft_outer",
    )(m_hi, m_lo, spec, x0, z, d_tiled)


def _twiddle(k1, n2, n):
    row = lax.broadcasted_iota(I32, (n2, LANES), 0)
    ang = ((row * k1) & (n - 1)).astype(F32) * (2.0 * math.pi / n)
    reps = HYENA_WIDTH // LANES
    c = jnp.concatenate([jnp.cos(ang)] * reps, axis=1)
    s = jnp.concatenate([jnp.sin(ang)] * reps, axis=1)
    return c, s


def _spec_filter_kernel(fh_ref, fl_ref, a_ref, scale_ref, o_ref, *, n):
    n2 = a_ref.shape[3]
    c, s = _twiddle(pl.program_id(0), n2, n)
    ar, ai = a_ref[0, 0, 0], a_ref[0, 1, 0]
    x = jnp.concatenate([ar * c + ai * s, ai * c - ar * s], axis=0)
    y = _dot3(fh_ref[...], fl_ref[...], x) * scale_ref[...]
    o_ref[0, 0] = y[:n2]
    o_ref[1, 0] = y[n2:]


def filter_spectrum(inner_fwd, a, scale, n):
    _, _, n1, n2, c = a.shape
    fh, fl = inner_fwd
    return pl.pallas_call(
        functools.partial(_spec_filter_kernel, n=n),
        out_shape=jax.ShapeDtypeStruct((2, n1, n2, c), F32),
        grid=(n1,),
        in_specs=[pl.BlockSpec((2 * n2, 2 * n2), lambda k: (0, 0)),
                  pl.BlockSpec((2 * n2, 2 * n2), lambda k: (0, 0)),
                  pl.BlockSpec((1, 2, 1, n2, c), lambda k: (0, 0, k, 0, 0)),
                  pl.BlockSpec((1, c), lambda k: (0, 0))],
        out_specs=pl.BlockSpec((2, 1, n2, c), lambda k: (0, k, 0, 0)),
        compiler_params=_cp(("parallel",), VMEM_LIMIT),
        name="hyena_filter_spectrum",
    )(fh, fl, a, scale)


def _spec_conv_kernel(fh_ref, fl_ref, ih_ref, il_ref, a_ref, k_ref, o_ref, *, n):
    n2 = a_ref.shape[3]
    c, s = _twiddle(pl.program_id(1), n2, n)
    ar, ai = a_ref[0, 0, 0], a_ref[0, 1, 0]
    x = jnp.concatenate([ar * c + ai * s, ai * c - ar * s], axis=0)
    y = _dot3(fh_ref[...], fl_ref[...], x)
    yr, yi = y[:n2], y[n2:]
    kr, ki = k_ref[0, 0], k_ref[1, 0]
    p = jnp.concatenate([yr * kr - yi * ki, yr * ki + yi * kr], axis=0)
    w = _dot3(ih_ref[...], il_ref[...], p)
    wr, wi = w[:n2], w[n2:]
    o_ref[0, 0, 0] = wr * c - wi * s
    o_ref[0, 1, 0] = wr * s + wi * c


def spectrum_convolve(inner, a, kspec, n):
    p, _, n1, n2, c = a.shape
    (fh, fl), (ih, il) = inner
    msp = pl.BlockSpec((2 * n2, 2 * n2), lambda pi, k: (0, 0))
    return pl.pallas_call(
        functools.partial(_spec_conv_kernel, n=n),
        out_shape=jax.ShapeDtypeStruct((p, 2, n1, n2, c), F32),
        grid=(p, n1),
        in_specs=[msp, msp, msp, msp,
                  pl.BlockSpec((1, 2, 1, n2, c), lambda pi, k: (pi, 0, k, 0, 0)),
                  pl.BlockSpec((2, 1, n2, c), lambda pi, k: (0, k, 0, 0))],
        out_specs=pl.BlockSpec((1, 2, 1, n2, c), lambda pi, k: (pi, 0, k, 0, 0)),
        compiler_params=_cp(("parallel", "parallel"), VMEM_LIMIT),
        name="hyena_spectrum_convolve",
    )(fh, fl, ih, il, a, kspec)


def hyena_filter_spectrum(seq, p):
    n = 2 * seq
    n1, n2 = _fft_factors(n)
    c = HYENA_WIDTH
    taps, asum = hyena_filter_taps(seq, p['filt_w1'], p['filt_b1'], p['filt_w2'], p['filt_b2'],
                                   p['filt_w3'], p['filt_b3'], p['filt_w4'], p['filt_freq'])
    _, fwd_real, _ = _stage_outer_mats(n1)
    a = left_dft(fwd_real, taps.reshape(1, n1, n2 * c))
    inner_fwd, _ = _stage_inner_mats(n2)
    scale = 1.0 / (asum * n)
    return filter_spectrum(inner_fwd, a.reshape(1, 2, n1, n2, c), scale, n)


def hyena_long_conv(x0, z, kspec, hy_d):
    b, l, c = z.shape
    n = 2 * l
    n1, n2 = _fft_factors(n)
    fwd_pair, _, inv_pair = _stage_outer_mats(n1)
    inner = _stage_inner_mats(n2)
    zp = z.reshape(b // 2, n1, n2 * c)
    a = left_dft(fwd_pair, zp)
    w = spectrum_convolve(inner, a.reshape(b // 2, 2, n1, n2, c), kspec, n)
    d_tiled = jnp.tile(hy_d.reshape(1, c), (1, n2))
    y = left_idft_gate(inv_pair, w.reshape(b // 2, 2 * n1, n2 * c),
                       x0.reshape(b // 2, n1, n2 * c), zp, d_tiled)
    return y.reshape(b, l, c)


def _rope_table_kernel(inv_ref, cos_ref, sin_ref):
    tm = cos_ref.shape[0]
    pos = (pl.program_id(0) * tm + lax.broadcasted_iota(I32, (tm, LANES), 0)).astype(F32)
    ang = pos * inv_ref[...]
    lane = lax.broadcasted_iota(I32, (tm, LANES), 1)
    cos_ref[...] = jnp.cos(ang)
    sin_ref[...] = jnp.where(lane % HEAD_DIM < HEAD_DIM // 2, -jnp.sin(ang), jnp.sin(ang))


def rope_tables(seq):
    inv = ROPE_THETA ** (-jnp.arange(0, HEAD_DIM, 2, dtype=F32) / HEAD_DIM)
    inv128 = jnp.tile(inv, LANES // (HEAD_DIM // 2)).reshape(1, LANES)
    tm = 512
    sds = jax.ShapeDtypeStruct((seq, LANES), F32)
    return pl.pallas_call(
        _rope_table_kernel,
        out_shape=(sds, sds),
        grid=(seq // tm,),
        in_specs=[pl.BlockSpec((1, LANES), lambda i: (0, 0))],
        out_specs=(pl.BlockSpec((tm, LANES), lambda i: (i, 0)), pl.BlockSpec((tm, LANES), lambda i: (i, 0))),
        compiler_params=_cp(("parallel",), VMEM_LIMIT),
        name="rope_tables",
    )(inv128)


def _norm_rope(t, g, cos, sin_signed, ones_bd, scale):
    sq = t * t
    parts = []
    for c0 in range(0, ATTN_WIDTH, 256):
        hi, lo = _split(sq[:, c0:c0 + 256])
        parts.append(_dot(hi, ones_bd) + _dot(lo, ones_bd))
    ms = jnp.concatenate(parts, axis=1) * (1.0 / HEAD_DIM)
    tn = t * lax.rsqrt(ms + EPS) * g
    lane = lax.broadcasted_iota(I32, (t.shape[0], LANES), 1)
    first_half = lane % HEAD_DIM < HEAD_DIM // 2
    outs = []
    for c0 in range(0, ATTN_WIDTH, LANES):
        x = tn[:, c0:c0 + LANES]
        rot = jnp.where(first_half, pltpu.roll(x, LANES - HEAD_DIM // 2, 1), pltpu.roll(x, HEAD_DIM // 2, 1))
        outs.append((x * cos + rot * sin_signed) * scale)
    return jnp.concatenate(outs, axis=1)


def _qk_prep_kernel(qkv_ref, cos_ref, sin_ref, qg_ref, kg_ref, bd_ref, q_ref, k_ref, v_ref):
    w = ATTN_WIDTH
    cos, sin = cos_ref[...], sin_ref[...]
    bd = bd_ref[...]
    q_ref[0] = _norm_rope(qkv_ref[0, :, :w], qg_ref[...], cos, sin, bd, HEAD_DIM ** -0.5).astype(BF16)
    k_ref[0] = _norm_rope(qkv_ref[0, :, w:2 * w], kg_ref[...], cos, sin, bd, 1.0).astype(BF16)
    v_ref[0] = qkv_ref[0, :, 2 * w:].astype(BF16)


def qk_prep(qkv, cos, sin, q_g, k_g):
    b, l, _ = qkv.shape
    tm = 512
    w = ATTN_WIDTH
    head = np.arange(256) // HEAD_DIM
    ones_bd = jnp.asarray(head[:, None] == head[None, :], BF16)
    sds = jax.ShapeDtypeStruct((b, l, w), BF16)
    tile_g = lambda g: jnp.tile(g, w // HEAD_DIM).reshape(1, w)
    ospec = pl.BlockSpec((1, tm, w), lambda bi, i: (bi, i, 0))
    return pl.pallas_call(
        _qk_prep_kernel,
        out_shape=(sds, sds, sds),
        grid=(b, l // tm),
        in_specs=[pl.BlockSpec((1, tm, 3 * w), lambda bi, i: (bi, i, 0)),
                  pl.BlockSpec((tm, LANES), lambda bi, i: (i, 0)),
                  pl.BlockSpec((tm, LANES), lambda bi, i: (i, 0)),
                  pl.BlockSpec((1, w), lambda bi, i: (0, 0)),
                  pl.BlockSpec((1, w), lambda bi, i: (0, 0)),
                  pl.BlockSpec((256, 256), lambda bi, i: (0, 0))],
        out_specs=(ospec, ospec, ospec),
        compiler_params=_cp(("parallel", "parallel"), VMEM_LIMIT),
        name="qk_norm_rope",
    )(qkv, cos, sin, tile_g(q_g), tile_g(k_g), ones_bd)


def _band_attn_kernel(q_ref, kp_ref, kc_ref, kn_ref, vp_ref, vc_ref, vn_ref, o_ref, lse_ref, *, m_len):
    i = pl.program_id(2)
    ts = q_ref.shape[1]
    h = ATTN_HALF
    tq = min(ATTN_QUERY_TILE, ts)
    kk = jnp.concatenate([kp_ref[0], kc_ref[0], kn_ref[0]], axis=0)
    vv = jnp.concatenate([vp_ref[0], vc_ref[0], vn_ref[0]], axis=0)
    qi = lax.broadcasted_iota(I32, (tq, tq + 2 * h), 0)
    kj = lax.broadcasted_iota(I32, (tq, tq + 2 * h), 1)
    in_band = jnp.abs(kj - h - qi) <= h
    for q0 in range(0, ts, tq):
        q = q_ref[0, q0:q0 + tq, :]
        kpos = i * ts + q0 - h + kj
        mask = in_band & (kpos >= 0) & (kpos < m_len)
        outs, lses = [], []
        for hd in range(HEADS_PER_GROUP):
            sl = slice(hd * HEAD_DIM, (hd + 1) * HEAD_DIM)
            s = lax.dot_general(q[:, sl], kk[q0:q0 + tq + 2 * h, sl], (((1,), (1,)), ((), ())),
                                preferred_element_type=F32)
            s = jnp.where(mask, s, NEG_BIG)
            mx = jnp.max(s, axis=1, keepdims=True)
            p = jnp.exp(s - mx)
            den = jnp.sum(p, axis=1, keepdims=True)
            o = _dot(p.astype(BF16), vv[q0:q0 + tq + 2 * h, sl]) / den
            outs.append(o)
            lses.append(jnp.broadcast_to(mx + jnp.log(den), (tq, HEAD_DIM)))
        o_ref[0, q0:q0 + tq, :] = jnp.concatenate(outs, axis=1)
        lse_ref[0, q0:q0 + tq, :] = jnp.concatenate(lses, axis=1)


def band_attention(q, k, v, group, dil):
    b, l, w = q.shape
    m_len = l // dil
    tq = min(ATTN_STEP_ROWS, m_len)
    hb = ATTN_HALF
    gw = ATTN_OUT
    nhb = m_len // hb
    view = lambda t: t.reshape(b, m_len, dil * w)
    col = lambda r: r * (w // gw) + group
    cur = pl.BlockSpec((1, tq, gw), lambda bi, r, i: (bi, i, col(r)))
    prev = pl.BlockSpec((1, hb, gw), lambda bi, r, i: (bi, jnp.maximum(i * (tq // hb) - 1, 0), col(r)))
    nxt = pl.BlockSpec((1, hb, gw), lambda bi, r, i: (bi, jnp.minimum((i + 1) * (tq // hb), nhb - 1), col(r)))
    ospec = pl.BlockSpec((1, tq, gw), lambda bi, r, i: (bi, i, r))
    sds = jax.ShapeDtypeStruct((b, m_len, dil * gw), F32)
    qv, kv, vv = view(q), view(k), view(v)
    o, lse = pl.pallas_call(
        functools.partial(_band_attn_kernel, m_len=m_len),
        out_shape=(sds, sds),
        grid=(b, dil, m_len // tq),
        in_specs=[cur, prev, cur, nxt, prev, cur, nxt],
        out_specs=(ospec, ospec),
        compiler_params=_cp(("parallel", "parallel", "parallel"), VMEM_LIMIT),
        name=f"band_attention_d{dil}",
    )(qv, kv, kv, kv, vv, vv, vv)
    return o.reshape(b, l, gw), lse.reshape(b, l, gw)


def _pooled(a_cur, a_prev, a_next, i, last, seq):
    tm = a_cur.shape[0]
    hl = POOL_HALO
    prev = jnp.where(i > 0, a_prev, 0.0)
    nxt = jnp.where(i < last, a_next, 0.0)
    ext = jnp.concatenate([prev, a_cur, nxt], axis=0)
    rows = ext.shape[0]
    lane = lax.broadcasted_iota(I32, (1, POOL_WIDTH), 1)
    pos = i * tm + lax.broadcasted_iota(I32, (tm, 1), 0)
    trail = ext
    win = None
    count = None
    for gi, w in enumerate(POOL_WINDOWS):
        trail = trail + pltpu.roll(trail, w // 2, 0)
        left, right = w // 2, w - 1 - w // 2
        centred = pltpu.roll(trail, rows - right, 0) if right else trail
        centred = centred[hl:hl + tm]
        cnt = (jnp.minimum(pos + right, seq - 1) - jnp.maximum(pos - left, 0) + 1).astype(F32)
        if win is None:
            win, count = centred, jnp.broadcast_to(cnt, (tm, POOL_WIDTH))
        else:
            sel = lane >= gi * POOL_GROUP
            win = jnp.where(sel, centred, win)
            count = jnp.where(sel, cnt, count)
    return win / count - a_cur


def _merge_kernel(a_ref, ap_ref, an_ref, yb_ref, o0_ref, o1_ref, o2_ref, l0_ref, l1_ref, l2_ref,
                  gt_ref, x_ref, mod_ref, pw_ref, ps_ref, wa_ref, wb_ref, wc_ref, wo_ref, out_ref, *, seq):
    i = pl.program_id(1)
    last = pl.num_programs(1) - 1
    d = D_MODEL
    pooled = _pooled(a_ref[0], ap_ref[0], an_ref[0], i, last, seq)
    ya = _dot(pooled.astype(BF16), pw_ref[...]) * ps_ref[...]
    l0, l1, l2 = l0_ref[0], l1_ref[0], l2_ref[0]
    mx = jnp.maximum(jnp.maximum(l0, l1), l2)
    e0, e1, e2 = jnp.exp(l0 - mx), jnp.exp(l1 - mx), jnp.exp(l2 - mx)
    yc = (e0 * o0_ref[0] + e1 * o1_ref[0] + e2 * o2_ref[0]) / (e0 + e1 + e2)
    gate = lambda k: jax.nn.sigmoid(gt_ref[0, :, k * d:(k + 1) * d].astype(F32))
    merged = (gate(0) * _dot(ya.astype(BF16), wa_ref[...])
              + gate(1) * _dot(yb_ref[0].astype(BF16), wb_ref[...])
              + gate(2) * _dot(yc.astype(BF16), wc_ref[...]))
    out_ref[0] = x_ref[0] + mod_ref[0, 2:3, :] * _dot(merged.astype(BF16), wo_ref[...])


def merge_branches(a, yb, attn, gates, x, mod, pool_bd, pool_scale, wa, wb, wc, wo):
    b, l, d = x.shape
    tm = 256
    hl = POOL_HALO
    nh = l // hl
    (o0, l0), (o1, l1), (o2, l2) = attn
    row = lambda w: pl.BlockSpec((1, tm, w), lambda bi, i: (bi, i, 0))
    full = lambda s: pl.BlockSpec(s, lambda bi, i: (0, 0))
    return pl.pallas_call(
        functools.partial(_merge_kernel, seq=l),
        out_shape=jax.ShapeDtypeStruct((b, l, d), F32),
        grid=(b, l // tm),
        in_specs=[row(POOL_WIDTH),
                  pl.BlockSpec((1, hl, POOL_WIDTH), lambda bi, i: (bi, jnp.maximum(i * (tm // hl) - 1, 0), 0)),
                  pl.BlockSpec((1, hl, POOL_WIDTH), lambda bi, i: (bi, jnp.minimum((i + 1) * (tm // hl), nh - 1), 0)),
                  row(HYENA_WIDTH), row(ATTN_OUT), row(ATTN_OUT), row(ATTN_OUT),
                  row(ATTN_OUT), row(ATTN_OUT), row(ATTN_OUT), row(3 * d), row(d),
                  pl.BlockSpec((1, N_MOD, d), lambda bi, i: (bi, 0, 0)),
                  full((POOL_WIDTH, POOL_WIDTH)), full((1, POOL_WIDTH)), full((POOL_WIDTH, d)),
                  full((HYENA_WIDTH, d)), full((ATTN_OUT, d)), full((d, d))],
        out_specs=row(d),
        compiler_params=_cp(("parallel", "parallel"), VMEM_LIMIT),
        name="merge_branches",
    )(a, a, a, yb, o0, o1, o2, l0, l1, l2, gates, x, mod, pool_bd, pool_scale, wa, wb, wc, wo)


TOKEN_TILE_ROWS = D_MODEL // LANES


def _store_token_tiles(ref, val):
    tm = val.shape[0]
    for j in range(TOKEN_TILE_ROWS):
        ref[pl.ds(j, tm, stride=TOKEN_TILE_ROWS), :] = val[:, j * LANES:(j + 1) * LANES]


def _load_token_tiles(ref):
    tm = ref.shape[0] // TOKEN_TILE_ROWS
    return jnp.concatenate([ref[pl.ds(j, tm, stride=TOKEN_TILE_ROWS), :] for j in range(TOKEN_TILE_ROWS)],
                           axis=1)


def _router_kernel(x_ref, mod_ref, g_ref, wr_ref, br_ref, tri_ref, h_ref, idx_ref, gw_ref, rank_ref,
                   cnt_ref, carry_ref):
    first = (pl.program_id(0) == 0) & (pl.program_id(1) == 0)

    @pl.when(first)
    def _():
        carry_ref[...] = jnp.zeros_like(carry_ref)

    h = _modulated_norm(x_ref[0], g_ref[...], mod_ref[0, 4:5, :], mod_ref[0, 3:4, :])
    _store_token_tiles(h_ref, h)
    w_hi, w_lo = _split(wr_ref[...])
    logits = _dot3r(h, w_hi, w_lo) + br_ref[...]
    tm, ne = logits.shape
    lane = lax.broadcasted_iota(I32, (tm, ne), 1).astype(F32)
    lane_out = lax.broadcasted_iota(I32, (tm, LANES), 1)
    work = logits
    vals, idxs = [], []
    member = jnp.zeros((tm, ne), F32)
    for _ in range(TOP_K):
        mx = jnp.max(work, axis=1, keepdims=True)
        idx = jnp.min(jnp.where(work == mx, lane, float(ne)), axis=1, keepdims=True)
        hit = lane == idx
        member = jnp.where(hit, 1.0, member)
        work = jnp.where(hit, -jnp.inf, work)
        vals.append(mx)
        idxs.append(idx)
    exps = [jnp.exp(v - vals[0]) for v in vals]
    den = exps[0] + exps[1] + exps[2] + exps[3]
    prefix = _dot(tri_ref[...], member.astype(BF16)) + carry_ref[...]
    idx_out = jnp.zeros((tm, LANES), I32)
    gw_out = jnp.zeros((tm, LANES), F32)
    rank_out = jnp.zeros((tm, LANES), I32)
    for k in range(TOP_K):
        rk = jnp.sum(jnp.where(lane == idxs[k], prefix, 0.0), axis=1, keepdims=True)
        idx_out = jnp.where(lane_out == k, idxs[k].astype(I32), idx_out)
        gw_out = jnp.where(lane_out == k, exps[k] / den, gw_out)
        rank_out = jnp.where(lane_out == k, rk.astype(I32), rank_out)
    idx_ref[0] = idx_out
    gw_ref[0] = gw_out
    rank_ref[0] = rank_out
    carry_ref[...] += jnp.sum(member, axis=0, keepdims=True)
    cnt_ref[...] = carry_ref[...]


def moe_router(x, mod, g2, w_router, b_router):
    b, l, d = x.shape
    tm = 256
    nl = l // tm
    ne = N_EXPERTS
    ttr = TOKEN_TILE_ROWS
    tri = jnp.asarray(np.tril(np.ones((tm, tm), np.float32), -1), BF16)
    row = lambda w: pl.BlockSpec((1, tm, w), lambda bi, i: (bi, i, 0))
    full = lambda s: pl.BlockSpec(s, lambda bi, i: (0, 0))
    return pl.pallas_call(
        _router_kernel,
        out_shape=(jax.ShapeDtypeStruct((b * l * ttr, LANES), F32),
                   jax.ShapeDtypeStruct((b, l, LANES), I32),
                   jax.ShapeDtypeStruct((b, l, LANES), F32),
                   jax.ShapeDtypeStruct((b, l, LANES), I32),
                   jax.ShapeDtypeStruct((1, ne), F32)),
        grid=(b, l // tm),
        in_specs=[row(d), pl.BlockSpec((1, N_MOD, d), lambda bi, i: (bi, 0, 0)), full((1, d)),
                  full((d, ne)), full((1, ne)), full((tm, tm))],
        out_specs=(pl.BlockSpec((tm * ttr, LANES), lambda bi, i: (bi * nl + i, 0)),
                   row(LANES), row(LANES), row(LANES), full((1, ne))),
        scratch_shapes=[pltpu.VMEM((1, ne), F32)],
        compiler_params=_cp(("arbitrary", "arbitrary"), VMEM_LIMIT),
        name="moe_router",
    )(x, mod, g2.reshape(1, d), w_router, b_router.reshape(1, ne), tri)


def _tile_copy(src, src_row, dst, dst_row, sem):
    n = TOKEN_TILE_ROWS
    return pltpu.make_async_copy(src.at[pl.ds(src_row, n), :], dst.at[pl.ds(dst_row, n), :], sem)


def _start_gather(h_hbm, tok_ref, xbuf, sem):
    n = TOKEN_TILE_ROWS
    for r in range(xbuf.shape[0] // n):
        _tile_copy(h_hbm, pl.multiple_of(tok_ref[0, 0, r], n), xbuf, r * n, sem).start()


def _start_scatter(obuf, slot_ref, out_hbm, sem):
    n = TOKEN_TILE_ROWS
    for r in range(obuf.shape[0] // n):
        _tile_copy(obuf, r * n, out_hbm, pl.multiple_of(slot_ref[0, 0, r], n), sem).start()


def _wait_block(hbm, buf, sem):
    pltpu.make_async_copy(hbm.at[pl.ds(0, buf.shape[0]), :], buf, sem).wait()


def _expert_ffn(x, w1_ref, b1_ref, w2_ref, b2_ref):
    f = D_MODEL
    u = _dot(x.astype(BF16), w1_ref[0]) + b1_ref[0]
    x_glu = jnp.minimum(u[:, :f], SWIGLU_LIMIT)
    x_lin = jnp.clip(u[:, f:], -SWIGLU_LIMIT, SWIGLU_LIMIT)
    act = x_glu * jax.nn.sigmoid(SWIGLU_ALPHA * x_glu) * (x_lin + 1.0)
    return _dot(act.astype(BF16), w2_ref[0]) + b2_ref[0]


def _expert_kernel(blk_e_ref, tok_ref, tokn_ref, slot_ref, h_hbm, w1_ref, b1_ref, w2_ref, b2_ref, out_hbm,
                   xa, xb, oa, ob, gsem, ssem):
    i = pl.program_id(0)
    last = pl.num_programs(0) - 1

    @pl.when(i == 0)
    def _():
        _start_gather(h_hbm, tok_ref, xa, gsem.at[0])

    def step(par, x_cur, x_next, o_cur, o_prev):
        @pl.when((i % 2 == par) & (i >= 2))
        def _():
            _wait_block(out_hbm, o_cur, ssem.at[par])

        @pl.when(i % 2 == par)
        def _():
            _wait_block(h_hbm, x_cur, gsem.at[par])
            _start_gather(h_hbm, tokn_ref, x_next, gsem.at[1 - par])
            _store_token_tiles(o_cur, _expert_ffn(_load_token_tiles(x_cur), w1_ref, b1_ref, w2_ref, b2_ref))
            _start_scatter(o_cur, slot_ref, out_hbm, ssem.at[par])

        @pl.when((i == last) & (i % 2 == par))
        def _():
            _wait_block(h_hbm, x_next, gsem.at[1 - par])
            _wait_block(out_hbm, o_cur, ssem.at[par])
            _wait_block(out_hbm, o_prev, ssem.at[1 - par])

    step(0, xa, xb, oa, ob)
    step(1, xb, xa, ob, oa)


def moe_experts(h_flat, row_tok, row_slot, blk_e, n_slots, w1_bf, b1, w2_bf, b2):
    nblk, _, bm = row_tok.shape
    assert nblk >= 2
    ne, d, f2 = w1_bf.shape
    ttr = TOKEN_TILE_ROWS
    meta = blk_e.astype(I32)
    smem = lambda imap: pl.BlockSpec((1, 1, bm), imap, memory_space=pltpu.SMEM)
    grid_spec = pltpu.PrefetchScalarGridSpec(
        num_scalar_prefetch=1,
        grid=(nblk,),
        in_specs=[smem(lambda i, m: (i, 0, 0)),
                  smem(lambda i, m: (jnp.minimum(i + 1, nblk - 1), 0, 0)),
                  smem(lambda i, m: (i, 0, 0)),
                  pl.BlockSpec(memory_space=pl.ANY),
                  pl.BlockSpec((1, d, f2), lambda i, m: (m[i], 0, 0)),
                  pl.BlockSpec((1, 1, f2), lambda i, m: (m[i], 0, 0)),
                  pl.BlockSpec((1, f2 // 2, d), lambda i, m: (m[i], 0, 0)),
                  pl.BlockSpec((1, 1, d), lambda i, m: (m[i], 0, 0))],
        out_specs=pl.BlockSpec(memory_space=pl.ANY),
        scratch_shapes=[pltpu.VMEM((bm * ttr, LANES), F32), pltpu.VMEM((bm * ttr, LANES), F32),
                        pltpu.VMEM((bm * ttr, LANES), F32), pltpu.VMEM((bm * ttr, LANES), F32),
                        pltpu.SemaphoreType.DMA((2,)), pltpu.SemaphoreType.DMA((2,))],
    )
    return pl.pallas_call(
        _expert_kernel,
        out_shape=jax.ShapeDtypeStruct((n_slots * ttr, LANES), F32),
        grid_spec=grid_spec,
        compiler_params=pltpu.CompilerParams(dimension_semantics=("arbitrary",), vmem_limit_bytes=VMEM_LIMIT,
                                             disable_bounds_checks=True),
        name="moe_experts",
    )(meta, row_tok, row_tok, row_slot, h_flat, w1_bf, b1.reshape(ne, 1, f2), w2_bf, b2.reshape(ne, 1, d))


def _combine_kernel(x_ref, s0_ref, s1_ref, s2_ref, s3_ref, gw_ref, mod_ref, o_ref):
    gw = gw_ref[0]
    acc = gw[:, 0:1] * _load_token_tiles(s0_ref)
    for k, s_ref in enumerate((s1_ref, s2_ref, s3_ref), start=1):
        acc = acc + gw[:, k:k + 1] * _load_token_tiles(s_ref)
    o_ref[0] = x_ref[0] + mod_ref[0, 5:6, :] * acc


def moe_combine(x, slots, gate_w, mod):
    b, l, d = x.shape
    tm = 256
    nl = l // tm
    nt = b * nl
    plane = lambda k: pl.BlockSpec((tm * TOKEN_TILE_ROWS, LANES), lambda bi, i: (k * nt + bi * nl + i, 0))
    row = lambda w: pl.BlockSpec((1, tm, w), lambda bi, i: (bi, i, 0))
    return pl.pallas_call(
        _combine_kernel,
        out_shape=jax.ShapeDtypeStruct((b, l, d), F32),
        grid=(b, nl),
        in_specs=[row(d), plane(0), plane(1), plane(2), plane(3), row(LANES),
                  pl.BlockSpec((1, N_MOD, d), lambda bi, i: (bi, 0, 0))],
        out_specs=row(d),
        compiler_params=_cp(("parallel", "parallel"), VMEM_LIMIT),
        name="moe_combine",
    )(x, slots, slots, slots, slots, gate_w, mod)


def moe_block(x, mod, p):
    b, l, d = x.shape
    t = b * l
    bm = MOE_ROWS
    n_assign = t * TOP_K
    h, top_i, gate_w, rank, counts = moe_router(x, mod, p['norm2_g'], p['w_router'], p['b_router'])
    top_i = top_i.reshape(t, LANES)[:, :TOP_K]
    rank = rank.reshape(t, LANES)[:, :TOP_K]
    counts = counts.reshape(N_EXPERTS).astype(I32)
    pcounts = (counts + bm - 1) // bm * bm
    pends = jnp.cumsum(pcounts)
    pstarts = pends - pcounts
    dest = pstarts[top_i] + rank
    nblk = n_assign // bm + N_EXPERTS
    n_rows = nblk * bm
    slot_tk = (jnp.arange(TOP_K, dtype=I32)[None, :] * t + jnp.arange(t, dtype=I32)[:, None])
    row_slot = jnp.full((n_rows,), -1, I32).at[dest.reshape(-1)].set(
        slot_tk.reshape(-1), unique_indices=True, mode='promise_in_bounds')
    is_pad = row_slot < 0
    row_tok = jnp.where(is_pad, 0, row_slot % t)
    row_slot = jnp.where(is_pad, n_assign - 1 + jnp.cumsum(is_pad.astype(I32)), row_slot)
    blk_start = jnp.arange(nblk, dtype=I32) * bm
    blk_e = jnp.minimum(jnp.sum((pends[None, :] <= blk_start[:, None]).astype(I32), axis=1), N_EXPERTS - 1)
    ttr = TOKEN_TILE_ROWS
    slots = moe_experts(h, (row_tok * ttr).reshape(nblk, 1, bm), (row_slot * ttr).reshape(nblk, 1, bm), blk_e,
                        n_rows, p['w_mlp1'], p['b_mlp1'], p['w_mlp2'], p['b_mlp2'])
    return moe_combine(x, slots, gate_w, mod)


def token_mixer_block(x, mod, p, cos, sin, kspec):
    a, hy, qkv, gates = in_projection(x, mod, p['norm1_g'], p['w_in'], p['b_in'])
    x0, z = hyena_pre(hy, p['hy_conv_w'], p['hy_conv_b'])
    yb = hyena_long_conv(x0, z, kspec, p['hy_d'])
    q, k, v = qk_prep(qkv, cos, sin, p['q_norm_g'], p['k_norm_g'])
    attn = [band_attention(q, k, v, gi, dil) for gi, (_, dil) in enumerate(ATTN_GROUPS)]
    return merge_branches(a, yb, attn, gates, x, mod, p['pool_bd'], p['pool_scale'],
                          p['w_branch_a'], p['w_branch_b'], p['w_branch_c'], p['w_o'])


def _pool_block_diag(pool_w):
    g = POOL_GROUP
    bd = jnp.zeros((POOL_WIDTH, POOL_WIDTH), F32)
    for gi in range(len(POOL_WINDOWS)):
        bd = bd.at[gi * g:(gi + 1) * g, gi * g:(gi + 1) * g].set(pool_w[gi])
    return bd.astype(BF16)


def kernel(x_prompt, x_sample, c_prompt, c_sample, ada_w, ada_b, norm1_g, w_in, b_in, pool_w, pool_scale, hy_conv_w, hy_conv_b, filt_w1, filt_b1, filt_w2, filt_b2, filt_w3, filt_b3, filt_w4, filt_freq, hy_d, q_norm_g, k_norm_g, w_branch_a, w_branch_b, w_branch_c, w_o, norm2_g, w_router, b_router, w_mlp1, b_mlp1, w_mlp2, b_mlp2):
    depth = ada_w.shape[0]
    d = D_MODEL
    groups = [(x_prompt, c_prompt), (x_sample, c_sample)]
    nb = [x.shape[0] for x, _ in groups]
    c_all = jnp.concatenate([c for _, c in groups], axis=0)
    pad = (-c_all.shape[0]) % 8
    c_all = jnp.pad(c_all, ((0, pad), (0, 0)))
    mod_all = ada_modulation(c_all, ada_w, ada_b)
    tables = {}
    for x, _ in groups:
        l = x.shape[1]
        if l not in tables:
            tables[l] = rope_tables(l)
    ys = [x for x, _ in groups]
    for layer in range(depth):
        p = {
            'norm1_g': norm1_g[layer], 'w_in': w_in[layer].astype(BF16), 'b_in': b_in[layer],
            'pool_bd': _pool_block_diag(pool_w[layer]), 'pool_scale': pool_scale[layer].reshape(1, POOL_WIDTH),
            'hy_conv_w': hy_conv_w[layer], 'hy_conv_b': hy_conv_b[layer],
            'filt_w1': filt_w1[layer], 'filt_b1': filt_b1[layer], 'filt_w2': filt_w2[layer],
            'filt_b2': filt_b2[layer], 'filt_w3': filt_w3[layer], 'filt_b3': filt_b3[layer],
            'filt_w4': filt_w4[layer], 'filt_freq': filt_freq[layer], 'hy_d': hy_d[layer],
            'q_norm_g': q_norm_g[layer], 'k_norm_g': k_norm_g[layer],
            'w_branch_a': w_branch_a[layer].astype(BF16), 'w_branch_b': w_branch_b[layer].astype(BF16),
            'w_branch_c': w_branch_c[layer].astype(BF16), 'w_o': w_o[layer].astype(BF16),
            'norm2_g': norm2_g[layer], 'w_router': w_router[layer], 'b_router': b_router[layer],
            'w_mlp1': w_mlp1[layer].astype(BF16), 'b_mlp1': b_mlp1[layer],
            'w_mlp2': w_mlp2[layer].astype(BF16), 'b_mlp2': b_mlp2[layer],
        }
        kspecs = {}
        off = 0
        for gi, (x, _) in enumerate(groups):
            l = x.shape[1]
            if l not in kspecs:
                kspecs[l] = hyena_filter_spectrum(l, p)
            mod = mod_all[layer, off:off + nb[gi]].reshape(nb[gi], N_MOD, d)
            off += nb[gi]
            cos, sin = tables[l]
            x1 = token_mixer_block(ys[gi], mod, p, cos, sin, kspecs[l])
            ys[gi] = moe_block(x1, mod, p)
    return tuple(ys)
```
